```python
import math
import jax, jax.numpy as jnp
from jax import lax
import numpy as np

D_MODEL = 1024
BATCH = 1
SEQ = 16384
DEPTH = 2

CHUNK = 64
Q_BLOCK = 128
N_MIXERS = 2
ATTN_HEADS = 8
ATTN_HEAD_DIM = 64
ATTN_WIDTH = ATTN_HEADS * 2 * ATTN_HEAD_DIM
ROPE_DIM = ATTN_HEAD_DIM // 4
ROPE_THETA = 500000.0
CONV_WIDTH = 3
CONV_CH = D_MODEL
D_FF = -(-8 * D_MODEL // (3 * 256)) * 256
ALPHA = (2 * DEPTH) ** 0.25
BETA = (8 * DEPTH) ** -0.25
N_ATTN_LAYERS = (DEPTH + 1) // 2
N_CONV_LAYERS = DEPTH // 2
LN_EPS = 1e-5
NEG_INF = -1e30

kernel_name = "hybrid_diffattn_shortconv_deepnorm"


def layer_norm(x, g, b):
    xf = x.astype(jnp.float32)
    mu = jnp.mean(xf, axis=-1, keepdims=True)
    xc = xf - mu
    var = jnp.mean(xc * xc, axis=-1, keepdims=True)
    y = xc * lax.rsqrt(var + LN_EPS) * g.astype(jnp.float32) + b.astype(jnp.float32)
    return y.astype(x.dtype)


def rms_norm(x, g):
    xf = x.astype(jnp.float32)
    y = xf * lax.rsqrt(jnp.mean(xf * xf, axis=-1, keepdims=True) + LN_EPS) * g.astype(jnp.float32)
    return y.astype(x.dtype)


def rope_tables(seq):
    pos = jnp.arange(seq, dtype=jnp.float32)
    inv = ROPE_THETA ** (-jnp.arange(0, ROPE_DIM, 2, dtype=jnp.float32) / ROPE_DIM)
    ang = pos[:, None] * inv[None, :]
    return jnp.cos(ang), jnp.sin(ang)


def partial_rotary(x, cos, sin):
    half = ROPE_DIM // 2
    c = cos[None, :, None, None, :].astype(x.dtype)
    s = sin[None, :, None, None, :].astype(x.dtype)
    x1 = x[..., :half]
    x2 = x[..., half:ROPE_DIM]
    rot = jnp.concatenate([x1 * c - x2 * s, x2 * c + x1 * s], axis=-1)
    return jnp.concatenate([rot, x[..., ROPE_DIM:]], axis=-1)


def lambda_init_fn(layer_idx):
    return 0.8 - 0.6 * math.exp(-0.3 * layer_idx)


def diff_attention(x, w_qkv, w_o, lq1, lk1, lq2, lk2, g_sub, lam_init):
    B, S, _ = x.shape
    H, Dh = ATTN_HEADS, ATTN_HEAD_DIM
    qkv = jnp.einsum('bsd,de->bse', x, w_qkv)
    q = qkv[..., :ATTN_WIDTH].reshape(B, S, H, 2, Dh)
    k = qkv[..., ATTN_WIDTH:2 * ATTN_WIDTH].reshape(B, S, H, 2, Dh)
    v = qkv[..., 2 * ATTN_WIDTH:].reshape(B, S, H, 2 * Dh)
    cos, sin = rope_tables(S)
    q = partial_rotary(q, cos, sin) * (Dh ** -0.5)
    k = partial_rotary(k, cos, sin)
    f32 = jnp.float32
    lam = (jnp.exp(jnp.sum(lq1.astype(f32) * lk1.astype(f32)))
           - jnp.exp(jnp.sum(lq2.astype(f32) * lk2.astype(f32))) + lam_init)
    nb = S // Q_BLOCK
    q_blocks = jnp.moveaxis(q.reshape(B, nb, Q_BLOCK, H, 2, Dh), 1, 0)
    key_chunk = jnp.arange(S) // CHUNK

    def one_block(args):
        q_blk, bi = args
        s = jnp.einsum('bqhcd,bkhcd->bhcqk', q_blk, k).astype(f32)
        q_chunk = (bi * Q_BLOCK + jnp.arange(Q_BLOCK)) // CHUNK
        mask = key_chunk[None, :] <= q_chunk[:, None]
        s = jnp.where(mask, s, NEG_INF)
        p = jax.nn.softmax(s, axis=-1)
        a = p[:, :, 0] - lam * p[:, :, 1]
        return jnp.einsum('bhqk,bkhe->bqhe', a.astype(v.dtype), v)

    o = lax.map(one_block, (q_blocks, jnp.arange(nb)))
    o = jnp.moveaxis(o, 0, 1).reshape(B, S, H, 2 * Dh)
    o = rms_norm(o, g_sub) * (1.0 - lam_init)
    return jnp.einsum('bse,ed->bsd', o.reshape(B, S, ATTN_WIDTH), w_o)


def short_gated_conv(x, w_in, w_conv, w_out):
    bcx = jnp.einsum('bsd,de->bse', x, w_in)
    b_gate = bcx[..., :CONV_CH]
    c_gate = bcx[..., CONV_CH:2 * CONV_CH]
    xt = bcx[..., 2 * CONV_CH:]
    u = c_gate * xt
    y = lax.conv_general_dilated(
        u, w_conv[:, None, :], window_strides=(1,),
        padding=[(CONV_WIDTH - 1, 0)],
        dimension_numbers=('NWC', 'WIO', 'NWC'),
        feature_group_count=CONV_CH)
    return jnp.einsum('bsc,cd->bsd', b_gate * y, w_out)


def swiglu(x, w_gate, w_up, w_down):
    h = jax.nn.silu(jnp.einsum('bsd,df->bsf', x, w_gate)) * jnp.einsum('bsd,df->bsf', x, w_up)
    return jnp.einsum('bsf,fd->bsd', h, w_down)


def setup_inputs(seed: int = 0) -> dict:
    key = jax.random.key(seed)
    ks = jax.random.split(key, 18)
    f32 = jnp.float32

    def nrm(k, shape, scale):
        return jax.random.normal(k, shape, f32) * scale

    NA, NC, Dh = N_ATTN_LAYERS, N_CONV_LAYERS, ATTN_HEAD_DIM
    return {
        'x': nrm(ks[0], (BATCH, SEQ, D_MODEL), 1.0),
        'attn_w_qkv': nrm(ks[1], (NA, D_MODEL, 3 * ATTN_WIDTH), D_MODEL ** -0.5),
        'attn_w_o': nrm(ks[2], (NA, ATTN_WIDTH, D_MODEL), ATTN_WIDTH ** -0.5 * BETA),
        'attn_lambda_q1': nrm(ks[3], (NA, Dh), 0.1),
        'attn_lambda_k1': nrm(ks[4], (NA, Dh), 0.1),
        'attn_lambda_q2': nrm(ks[5], (NA, Dh), 0.1),
        'attn_lambda_k2': nrm(ks[6], (NA, Dh), 0.1),
        'attn_subln_g': 1.0 + nrm(ks[7], (NA, 2 * Dh), 0.02),
        'conv_w_in': nrm(ks[8], (NC, D_MODEL, 3 * CONV_CH), D_MODEL ** -0.5),
        'conv_w': nrm(ks[9], (NC, CONV_WIDTH, CONV_CH), CONV_WIDTH ** -0.5),
        'conv_w_out': nrm(ks[10], (NC, CONV_CH, D_MODEL), CONV_CH ** -0.5 * BETA),
        'ffn_w_gate': nrm(ks[11], (DEPTH, D_MODEL, D_FF), D_MODEL ** -0.5),
        'ffn_w_up': nrm(ks[12], (DEPTH, D_MODEL, D_FF), D_MODEL ** -0.5),
        'ffn_w_down': nrm(ks[13], (DEPTH, D_FF, D_MODEL), D_FF ** -0.5 * BETA),
        'ln1_g': 1.0 + nrm(ks[14], (DEPTH, D_MODEL), 0.02),
        'ln1_b': nrm(ks[15], (DEPTH, D_MODEL), 0.02),
        'ln2_g': 1.0 + nrm(ks[16], (DEPTH, D_MODEL), 0.02),
        'ln2_b': nrm(ks[17], (DEPTH, D_MODEL), 0.02),
    }


def reference(x, attn_w_qkv, attn_w_o, attn_lambda_q1, attn_lambda_k1, attn_lambda_q2,
              attn_lambda_k2, attn_subln_g, conv_w_in, conv_w, conv_w_out,
              ffn_w_gate, ffn_w_up, ffn_w_down, ln1_g, ln1_b, ln2_g, ln2_b):
    h = x
    for i in range(DEPTH):
        j = i // N_MIXERS
        if i % N_MIXERS == 0:
            m = diff_attention(h, attn_w_qkv[j], attn_w_o[j], attn_lambda_q1[j],
                               attn_lambda_k1[j], attn_lambda_q2[j], attn_lambda_k2[j],
                               attn_subln_g[j], lambda_init_fn(i))
        else:
            m = short_gated_conv(h, conv_w_in[j], conv_w[j], conv_w_out[j])
        h = layer_norm(ALPHA * h + m, ln1_g[i], ln1_b[i])
        h = layer_norm(ALPHA * h + swiglu(h, ffn_w_gate[i], ffn_w_up[i], ffn_w_down[i]),
                       ln2_g[i], ln2_b[i])
    return h
```

```python
import functools
import math

import jax
import jax.numpy as jnp
from jax import lax
from jax.experimental import pallas as pl
from jax.experimental.pallas import tpu as pltpu

F32 = jnp.float32
BF16 = jnp.bfloat16

CHUNK = 64
N_HEADS = 8
HEAD_DIM = 64
HEAD_W = 2 * HEAD_DIM
ROPE_DIM = HEAD_DIM // 4
ROPE_HALF = ROPE_DIM // 2
ROPE_THETA = 500000.0
CONV_WIDTH = 3
DEPTH = 2
ALPHA = (2 * DEPTH) ** 0.25
LN_EPS = 1e-5
NEG_INF = -1e30
LAMBDA_INIT_0 = 0.8 - 0.6 * math.exp(-0.3 * 0)

V7X_VMEM_LIMIT_BYTES = 56 * 1024 * 1024
MXU_WIDTH = 256


def _resident(shape):
    return pl.BlockSpec(shape, lambda *_: (0,) * len(shape), pipeline_mode=pl.Buffered(1))


def _layer_norm(x, g, b):
    mu = jnp.mean(x, axis=-1, keepdims=True)
    xc = x - mu
    var = jnp.mean(xc * xc, axis=-1, keepdims=True)
    return xc * lax.rsqrt(var + LN_EPS) * g + b


def _qkv_kernel(x_ref, w_ref, inv_ref, qs_ref, kt_ref, v_ref, *, tm, width):
    i = pl.program_id(0)
    xb = x_ref[...].astype(BF16)

    row = lax.broadcasted_iota(jnp.int32, (tm, HEAD_W), 0)
    lane = lax.broadcasted_iota(jnp.int32, (tm, HEAD_W), 1)
    w = lane % HEAD_DIM
    ang = (i * tm + row).astype(F32) * inv_ref[...]
    cos, sin = jnp.cos(ang), jnp.sin(ang)
    c_tab = jnp.where(w < ROPE_DIM, cos, 1.0)
    s_up = jnp.where(w < ROPE_HALF, -sin, 0.0)
    s_dn = jnp.where((w >= ROPE_HALF) & (w < ROPE_DIM), sin, 0.0)
    first = lane < HEAD_DIM

    def rope(t):
        up = pltpu.roll(t, HEAD_W - ROPE_HALF, 1)
        dn = pltpu.roll(t, ROPE_HALF, 1)
        return t * c_tab + up * s_up + dn * s_dn

    def proj(col):
        return jnp.dot(xb, w_ref[:, col:col + MXU_WIDTH], preferred_element_type=F32)

    for c in range(width // MXU_WIDTH):
        q2 = proj(c * MXU_WIDTH)
        k2 = proj(width + c * MXU_WIDTH)
        v2 = proj(2 * width + c * MXU_WIDTH)
        v_ref[:, c * MXU_WIDTH:(c + 1) * MXU_WIDTH] = v2.astype(BF16)
        for half in range(MXU_WIDTH // HEAD_W):
            lo = c * MXU_WIDTH + half * HEAD_W
            qh = rope(q2[:, half * HEAD_W:(half + 1) * HEAD_W]) * (HEAD_DIM ** -0.5)
            qs_ref[0, :, lo:lo + HEAD_W] = jnp.where(first, qh, 0.0).astype(BF16)
            qs_ref[1, :, lo:lo + HEAD_W] = jnp.where(first, 0.0, qh).astype(BF16)
            kh = rope(k2[:, half * HEAD_W:(half + 1) * HEAD_W])
            kt_ref[lo:lo + HEAD_W, :] = kh.T.astype(BF16)


def _qkv_call(x2, w_qkv, inv_lane, *, tm):
    s, d = x2.shape
    width = w_qkv.shape[1] // 3
    return pl.pallas_call(
        functools.partial(_qkv_kernel, tm=tm, width=width),
        grid=(s // tm,),
        in_specs=[
            pl.BlockSpec((tm, d), lambda i: (i, 0)),
            _resident(w_qkv.shape),
            _resident(inv_lane.shape),
        ],
        out_specs=[
            pl.BlockSpec((2, tm, width), lambda i: (0, i, 0)),
            pl.BlockSpec((width, tm), lambda i: (0, i)),
            pl.BlockSpec((tm, width), lambda i: (i, 0)),
        ],
        out_shape=[
            jax.ShapeDtypeStruct((2, s, width), BF16),
            jax.ShapeDtypeStruct((width, s), BF16),
            jax.ShapeDtypeStruct((s, width), BF16),
        ],
        compiler_params=pltpu.CompilerParams(
            dimension_semantics=("arbitrary",), vmem_limit_bytes=V7X_VMEM_LIMIT_BYTES),
        name="qkv_rope",
    )(x2, w_qkv, inv_lane)


def _attn_kernel(lam_ref, g_ref, qs_ref, kt_ref, v_ref, o_ref, m_sc, l_sc, acc_sc, *, tq, tk, lam_init):
    i = pl.program_id(1)
    q2 = qs_ref[...].reshape(2 * tq, HEAD_W)

    m_sc[...] = jnp.full(m_sc.shape, NEG_INF, F32)
    l_sc[...] = jnp.zeros(l_sc.shape, F32)
    acc_sc[...] = jnp.zeros(acc_sc.shape, F32)

    def tile(j, mask):
        start = pl.multiple_of(j * tk, tk)
        s = jnp.dot(q2, kt_ref[:, pl.ds(start, tk)], preferred_element_type=F32)
        if mask is not None:
            s = jnp.where(mask, s, NEG_INF)
        m_prev = m_sc[...]
        m_new = jnp.maximum(m_prev, jnp.max(s, axis=1, keepdims=True))
        p = jnp.exp(s - m_new)
        a = jnp.exp(m_prev - m_new)
        l_sc[...] = a * l_sc[...] + jnp.sum(p, axis=1, keepdims=True)
        acc_sc[...] = a * acc_sc[...] + jnp.dot(
            p.astype(BF16), v_ref[pl.ds(start, tk), :], preferred_element_type=F32)
        m_sc[...] = m_new

    n_sub = tq // tk
    n_full = i * n_sub

    def full_tile(j, carry):
        tile(j, None)
        return carry

    lax.fori_loop(0, n_full, full_tile, 0)

    row = lax.broadcasted_iota(jnp.int32, (2 * tq, tk), 0) % tq
    col = lax.broadcasted_iota(jnp.int32, (2 * tq, tk), 1)
    for r in range(n_sub):
        tile(n_full + r, (r * tk + col) // CHUNK <= row // CHUNK)

    lam_p = lam_ref[...]
    lam = (jnp.exp(jnp.sum(lam_p[0:1] * lam_p[1:2], axis=1, keepdims=True))
           - jnp.exp(jnp.sum(lam_p[2:3] * lam_p[3:4], axis=1, keepdims=True)) + lam_init)
    o_all = acc_sc[...] / l_sc[...]
    o = o_all[:tq] - lam * o_all[tq:]
    o = o * lax.rsqrt(jnp.mean(o * o, axis=-1, keepdims=True) + LN_EPS) * g_ref[...]
    o_ref[...] = (o * (1.0 - lam_init)).astype(o_ref.dtype)


def _attn_call(lam_p, g_sub, qs, kt, v, *, tq, tk, lam_init):
    _, s, width = qs.shape
    return pl.pallas_call(
        functools.partial(_attn_kernel, tq=tq, tk=tk, lam_init=lam_init),
        grid=(N_HEADS, s // tq),
        in_specs=[
            pl.BlockSpec(lam_p.shape, lambda h, i: (0, 0)),
            pl.BlockSpec(g_sub.shape, lambda h, i: (0, 0)),
            pl.BlockSpec((2, tq, HEAD_W), lambda h, i: (0, i, h)),
            pl.BlockSpec((HEAD_W, s), lambda h, i: (h, 0)),
            pl.BlockSpec((s, HEAD_W), lambda h, i: (0, h)),
        ],
        out_specs=pl.BlockSpec((tq, HEAD_W), lambda h, i: (i, h)),
        out_shape=jax.ShapeDtypeStruct((s, width), BF16),
        scratch_shapes=[
            pltpu.VMEM((2 * tq, 1), F32),
            pltpu.VMEM((2 * tq, 1), F32),
            pltpu.VMEM((2 * tq, HEAD_W), F32),
        ],
        compiler_params=pltpu.CompilerParams(
            dimension_semantics=("arbitrary", "arbitrary"), vmem_limit_bytes=V7X_VMEM_LIMIT_BYTES),
        name="diff_attn",
    )(lam_p, g_sub, qs, kt, v)


def _ffn_ln(h, wg_ref, wu_ref, wd_ref, act_sc, g2, b2):
    hb = h.astype(BF16)
    d_ff = wg_ref.shape[1]
    for c in range(d_ff // MXU_WIDTH):
        cols = slice(c * MXU_WIDTH, (c + 1) * MXU_WIDTH)
        gate = jnp.dot(hb, wg_ref[:, cols], preferred_element_type=F32)
        up = jnp.dot(hb, wu_ref[:, cols], preferred_element_type=F32)
        act_sc[:, cols] = (gate / (1.0 + jnp.exp(-gate)) * up).astype(BF16)
    y = jnp.dot(act_sc[...], wd_ref[...], preferred_element_type=F32)
    return _layer_norm(ALPHA * h + y, g2, b2)


def _post_attn_kernel(x_ref, o_ref, wo_ref, wg_ref, wu_ref, wd_ref, ln_ref, out_ref, act_sc):
    ln = ln_ref[...]
    m = jnp.dot(o_ref[...], wo_ref[...], preferred_element_type=F32)
    h = _layer_norm(ALPHA * x_ref[...] + m, ln[0:1], ln[1:2])
    out_ref[...] = _ffn_ln(h, wg_ref, wu_ref, wd_ref, act_sc, ln[2:3], ln[3:4])


def _post_attn_call(x2, o, w_o, w_gate, w_up, w_down, ln, *, tm):
    s, d = x2.shape
    d_ff = w_gate.shape[1]
    return pl.pallas_call(
        _post_attn_kernel,
        grid=(s // tm,),
        in_specs=[
            pl.BlockSpec((tm, d), lambda i: (i, 0)),
            pl.BlockSpec((tm, o.shape[1]), lambda i: (i, 0)),
            _resident(w_o.shape),
            _resident(w_gate.shape),
            _resident(w_up.shape),
            _resident(w_down.shape),
            _resident(ln.shape),
        ],
        out_specs=pl.BlockSpec((tm, d), lambda i: (i, 0)),
        out_shape=jax.ShapeDtypeStruct((s, d), F32),
        scratch_shapes=[pltpu.VMEM((tm, d_ff), BF16)],
        compiler_params=pltpu.CompilerParams(
            dimension_semantics=("arbitrary",), vmem_limit_bytes=V7X_VMEM_LIMIT_BYTES),
        name="attn_out_ffn",
    )(x2, o, w_o, w_gate, w_up, w_down, ln)


def _conv_layer_kernel(h_ref, win_ref, wc_ref, wout_ref, wg_ref, wu_ref, wd_ref, ln_ref, out_ref,
                       act_sc, tail_sc, *, tm, ch):
    i = pl.program_id(0)
    ln = ln_ref[...]
    h = h_ref[...]
    hb = h.astype(BF16)

    @pl.when(i == 0)
    def _():
        tail_sc[...] = jnp.zeros(tail_sc.shape, F32)

    b_gate = jnp.dot(hb, win_ref[:, 0:ch], preferred_element_type=F32)
    c_gate = jnp.dot(hb, win_ref[:, ch:2 * ch], preferred_element_type=F32)
    xt = jnp.dot(hb, win_ref[:, 2 * ch:3 * ch], preferred_element_type=F32)
    u = c_gate * xt

    row = lax.broadcasted_iota(jnp.int32, (tm, ch), 0)
    tail = tail_sc[...]
    wc = wc_ref[...]
    y = wc[CONV_WIDTH - 1:CONV_WIDTH] * u
    for d in range(1, CONV_WIDTH):
        shifted = pltpu.roll(u, d, 0)
        for t in range(d):
            src = CONV_WIDTH - 1 - d + t
            shifted = jnp.where(row == t, tail[src:src + 1], shifted)
        y = y + wc[CONV_WIDTH - 1 - d:CONV_WIDTH - d] * shifted
    tail_sc[...] = u[tm - (CONV_WIDTH - 1):]

    m = jnp.dot((b_gate * y).astype(BF16), wout_ref[...], preferred_element_type=F32)
    h1 = _layer_norm(ALPHA * h + m, ln[0:1], ln[1:2])
    out_ref[...] = _ffn_ln(h1, wg_ref, wu_ref, wd_ref, act_sc, ln[2:3], ln[3:4])


def _conv_layer_call(h, w_in, w_conv, w_out, w_gate, w_up, w_down, ln, *, tm):
    s, d = h.shape
    ch = w_out.shape[0]
    d_ff = w_gate.shape[1]
    return pl.pallas_call(
        functools.partial(_conv_layer_kernel, tm=tm, ch=ch),
        grid=(s // tm,),
        in_specs=[
            pl.BlockSpec((tm, d), lambda i: (i, 0)),
            _resident(w_in.shape),
            _resident(w_conv.shape),
            _resident(w_out.shape),
            _resident(w_gate.shape),
            _resident(w_up.shape),
            _resident(w_down.shape),
            _resident(ln.shape),
        ],
        out_specs=pl.BlockSpec((tm, d), lambda i: (i, 0)),
        out_shape=jax.ShapeDtypeStruct((s, d), F32),
        scratch_shapes=[pltpu.VMEM((tm, d_ff), BF16), pltpu.VMEM((CONV_WIDTH - 1, ch), F32)],
        compiler_params=pltpu.CompilerParams(
            dimension_semantics=("arbitrary",), vmem_limit_bytes=V7X_VMEM_LIMIT_BYTES),
        name="conv_ffn",
    )(h, w_in, w_conv, w_out, w_gate, w_up, w_down, ln)


def _rope_inv_lanes():
    inv = ROPE_THETA ** (-jnp.arange(0, ROPE_DIM, 2, dtype=F32) / ROPE_DIM)
    per_comp = jnp.concatenate([inv, inv, jnp.zeros((HEAD_DIM - ROPE_DIM,), F32)])
    return jnp.tile(per_comp, HEAD_W // HEAD_DIM)[None, :]


def kernel(x, attn_w_qkv, attn_w_o, attn_lambda_q1, attn_lambda_k1, attn_lambda_q2, attn_lambda_k2,
           attn_subln_g, conv_w_in, conv_w, conv_w_out, ffn_w_gate, ffn_w_up, ffn_w_down,
           ln1_g, ln1_b, ln2_g, ln2_b):
    b, s, d = x.shape
    assert b == 1 and attn_w_qkv.shape[0] == 1 and conv_w_in.shape[0] == 1
    tm = min(512, s)
    tq = tk = min(256, s)
    x2 = x.reshape(s, d)

    qs, kt, v = _qkv_call(x2, attn_w_qkv[0].astype(BF16), _rope_inv_lanes(), tm=tm)
    lam_p = jnp.concatenate([attn_lambda_q1, attn_lambda_k1, attn_lambda_q2, attn_lambda_k2], axis=0)
    o = _attn_call(lam_p, attn_subln_g, qs, kt, v, tq=tq, tk=tk, lam_init=LAMBDA_INIT_0)

    ln0 = jnp.stack([ln1_g[0], ln1_b[0], ln2_g[0], ln2_b[0]])
    h = _post_attn_call(x2, o, attn_w_o[0].astype(BF16), ffn_w_gate[0].astype(BF16),
                        ffn_w_up[0].astype(BF16), ffn_w_down[0].astype(BF16), ln0, tm=tm)

    ln1 = jnp.stack([ln1_g[1], ln1_b[1], ln2_g[1], ln2_b[1]])
    out = _conv_layer_call(h, conv_w_in[0].astype(BF16), conv_w[0], conv_w_out[0].astype(BF16),
                           ffn_w_gate[1].astype(BF16), ffn_w_up[1].astype(BF16),
                           ffn_w_down[1].astype(BF16), ln1, tm=tm)
    return out.reshape(b, s, d)
```

```python
import functools
import math

import jax
import jax.numpy as jnp
from jax import lax
from jax.experimental import pallas as pl
from jax.experimental.pallas import tpu as pltpu

F32 = jnp.float32
BF16 = jnp.bfloat16

CHUNK = 64
N_HEADS = 8
HEAD_DIM = 64
HEAD_W = 2 * HEAD_DIM
ROPE_DIM = HEAD_DIM // 4
ROPE_HALF = ROPE_DIM // 2
ROPE_THETA = 500000.0
CONV_WIDTH = 3
DEPTH = 2
ALPHA = (2 * DEPTH) ** 0.25
LN_EPS = 1e-5
NEG_INF = -1e30
LAMBDA_INIT_0 = 0.8 - 0.6 * math.exp(-0.3 * 0)

V7X_VMEM_LIMIT_BYTES = 56 * 1024 * 1024
MXU_WIDTH = 256


def _resident(shape):
    return pl.BlockSpec(shape, lambda *_: (0,) * len(shape), pipeline_mode=pl.Buffered(1))


def _layer_norm(x, g, b):
    mu = jnp.mean(x, axis=-1, keepdims=True)
    xc = x - mu
    var = jnp.mean(xc * xc, axis=-1, keepdims=True)
    return xc * lax.rsqrt(var + LN_EPS) * g + b


def _qkv_kernel(x_ref, w_ref, inv_ref, qt_ref, k_ref, vt_ref, *, tm, width):
    i = pl.program_id(0)
    xb = x_ref[...].astype(BF16)

    row = lax.broadcasted_iota(jnp.int32, (tm, HEAD_W), 0)
    lane = lax.broadcasted_iota(jnp.int32, (tm, HEAD_W), 1)
    w = lane % HEAD_DIM
    ang = (i * tm + row).astype(F32) * inv_ref[...]
    cos, sin = jnp.cos(ang), jnp.sin(ang)
    c_tab = jnp.where(w < ROPE_DIM, cos, 1.0)
    s_up = jnp.where(w < ROPE_HALF, -sin, 0.0)
    s_dn = jnp.where((w >= ROPE_HALF) & (w < ROPE_DIM), sin, 0.0)
    first_t = lax.broadcasted_iota(jnp.int32, (HEAD_W, tm), 0) < HEAD_DIM

    def rope(t):
        up = pltpu.roll(t, HEAD_W - ROPE_HALF, 1)
        dn = pltpu.roll(t, ROPE_HALF, 1)
        return t * c_tab + up * s_up + dn * s_dn

    def proj(col):
        return jnp.dot(xb, w_ref[:, col:col + MXU_WIDTH], preferred_element_type=F32)

    for c in range(width // MXU_WIDTH):
        q2 = proj(c * MXU_WIDTH)
        k2 = proj(width + c * MXU_WIDTH)
        v2 = proj(2 * width + c * MXU_WIDTH)
        for half in range(MXU_WIDTH // HEAD_W):
            lo = c * MXU_WIDTH + half * HEAD_W
            cols = slice(half * HEAD_W, (half + 1) * HEAD_W)
            qh_t = (rope(q2[:, cols]) * (HEAD_DIM ** -0.5)).T
            qt_ref[0, lo:lo + HEAD_W, :] = jnp.where(first_t, qh_t, 0.0).astype(BF16)
            qt_ref[1, lo:lo + HEAD_W, :] = jnp.where(first_t, 0.0, qh_t).astype(BF16)
            k_ref[:, lo:lo + HEAD_W] = rope(k2[:, cols]).astype(BF16)
            vt_ref[lo:lo + HEAD_W, :] = v2[:, cols].T.astype(BF16)


def _qkv_call(x2, w_qkv, inv_lane, *, tm):
    s, d = x2.shape
    width = w_qkv.shape[1] // 3
    return pl.pallas_call(
        functools.partial(_qkv_kernel, tm=tm, width=width),
        grid=(s // tm,),
        in_specs=[
            pl.BlockSpec((tm, d), lambda i: (i, 0)),
            _resident(w_qkv.shape),
            _resident(inv_lane.shape),
        ],
        out_specs=[
            pl.BlockSpec((2, width, tm), lambda i: (0, 0, i)),
            pl.BlockSpec((tm, width), lambda i: (i, 0)),
            pl.BlockSpec((width, tm), lambda i: (0, i)),
        ],
        out_shape=[
            jax.ShapeDtypeStruct((2, width, s), BF16),
            jax.ShapeDtypeStruct((s, width), BF16),
            jax.ShapeDtypeStruct((width, s), BF16),
        ],
        compiler_params=pltpu.CompilerParams(
            dimension_semantics=("arbitrary",), vmem_limit_bytes=V7X_VMEM_LIMIT_BYTES),
        name="qkv_rope",
    )(x2, w_qkv, inv_lane)


def _attn_kernel(lam_ref, g_ref, qt_ref, k_ref, vt_ref, o_ref, m_sc, l_sc, acc_sc, s_sc, p_sc, a_sc,
                 *, tq, tk, lam_init):
    i = pl.program_id(1)

    m_sc[...] = jnp.full(m_sc.shape, NEG_INF, F32)
    l_sc[...] = jnp.zeros(l_sc.shape, F32)
    acc_sc[...] = jnp.zeros(acc_sc.shape, F32)
    p_sc[...] = jnp.zeros(p_sc.shape, BF16)
    a_sc[...] = jnp.ones(a_sc.shape, F32)

    def scores(t):
        start = pl.multiple_of(t * tk, tk)
        k_t = k_ref[pl.ds(start, tk), :]
        for c in range(2):
            s_sc[c] = jnp.dot(k_t, qt_ref[c], preferred_element_type=F32)

    def softmax(mask):
        for c in range(2):
            s = s_sc[c]
            if mask is not None:
                s = jnp.where(mask, s, NEG_INF)
            m_prev = m_sc[c]
            m_new = jnp.maximum(m_prev, jnp.max(s, axis=0, keepdims=True))
            p = jnp.exp(s - m_new)
            a = jnp.exp(m_prev - m_new)
            l_sc[c] = a * l_sc[c] + jnp.sum(p, axis=0, keepdims=True)
            m_sc[c] = m_new
            p_sc[c] = p.astype(BF16)
            a_sc[c] = a

    def accumulate(t):
        start = pl.multiple_of(t * tk, tk)
        v_t = vt_ref[:, pl.ds(start, tk)]
        for c in range(2):
            acc_sc[c] = a_sc[c] * acc_sc[c] + jnp.dot(v_t, p_sc[c], preferred_element_type=F32)

    n_sub = tq // tk
    n_full = i * n_sub

    scores(0)

    def step(t, carry):
        accumulate(jnp.maximum(t - 1, 0))
        softmax(None)
        scores(t + 1)
        return carry

    lax.fori_loop(0, n_full, step, 0)

    key = lax.broadcasted_iota(jnp.int32, (tk, tq), 0)
    qry = lax.broadcasted_iota(jnp.int32, (tk, tq), 1)
    for r in range(n_sub):
        accumulate(jnp.maximum(n_full + r - 1, 0))
        softmax((r * tk + key) // CHUNK <= qry // CHUNK)
        if r + 1 < n_sub:
            scores(n_full + r + 1)
    accumulate(n_full + n_sub - 1)

    lam_p = lam_ref[...]
    lam = (jnp.exp(jnp.sum(lam_p[0:1] * lam_p[1:2], axis=1, keepdims=True))
           - jnp.exp(jnp.sum(lam_p[2:3] * lam_p[3:4], axis=1, keepdims=True)) + lam_init)
    o_t = acc_sc[0] / l_sc[0] - lam * (acc_sc[1] / l_sc[1])
    o = o_t.T
    o = o * lax.rsqrt(jnp.mean(o * o, axis=-1, keepdims=True) + LN_EPS) * g_ref[...]
    o_ref[...] = (o * (1.0 - lam_init)).astype(o_ref.dtype)


def _attn_call(lam_p, g_sub, qt, k, vt, *, tq, tk, lam_init):
    s, width = k.shape
    return pl.pallas_call(
        functools.partial(_attn_kernel, tq=tq, tk=tk, lam_init=lam_init),
        grid=(N_HEADS, s // tq),
        in_specs=[
            pl.BlockSpec(lam_p.shape, lambda h, i: (0, 0)),
            pl.BlockSpec(g_sub.shape, lambda h, i: (0, 0)),
            pl.BlockSpec((2, HEAD_W, tq), lambda h, i: (0, h, i)),
            pl.BlockSpec((s, HEAD_W), lambda h, i: (0, h)),
            pl.BlockSpec((HEAD_W, s), lambda h, i: (h, 0)),
        ],
        out_specs=pl.BlockSpec((tq, HEAD_W), lambda h, i: (i, h)),
        out_shape=jax.ShapeDtypeStruct((s, width), BF16),
        scratch_shapes=[
            pltpu.VMEM((2, 1, tq), F32),
            pltpu.VMEM((2, 1, tq), F32),
            pltpu.VMEM((2, HEAD_W, tq), F32),
            pltpu.VMEM((2, tk, tq), F32),
            pltpu.VMEM((2, tk, tq), BF16),
            pltpu.VMEM((2, 1, tq), F32),
        ],
        compiler_params=pltpu.CompilerParams(
            dimension_semantics=("arbitrary", "arbitrary"), vmem_limit_bytes=V7X_VMEM_LIMIT_BYTES),
        name="diff_attn",
    )(lam_p, g_sub, qt, k, vt)


def _ffn_ln(h, wg_ref, wu_ref, wd_ref, act_sc, g2, b2):
    hb = h.astype(BF16)
    d_ff = wg_ref.shape[1]
    for c in range(d_ff // MXU_WIDTH):
        cols = slice(c * MXU_WIDTH, (c + 1) * MXU_WIDTH)
        gate = jnp.dot(hb, wg_ref[:, cols], preferred_element_type=F32)
        up = jnp.dot(hb, wu_ref[:, cols], preferred_element_type=F32)
        act_sc[:, cols] = (gate / (1.0 + jnp.exp(-gate)) * up).astype(BF16)
    y = jnp.dot(act_sc[...], wd_ref[...], preferred_element_type=F32)
    return _layer_norm(ALPHA * h + y, g2, b2)


def _post_attn_kernel(x_ref, o_ref, wo_ref, wg_ref, wu_ref, wd_ref, ln_ref, out_ref, act_sc):
    ln = ln_ref[...]
    m = jnp.dot(o_ref[...], wo_ref[...], preferred_element_type=F32)
    h = _layer_norm(ALPHA * x_ref[...] + m, ln[0:1], ln[1:2])
    out_ref[...] = _ffn_ln(h, wg_ref, wu_ref, wd_ref, act_sc, ln[2:3], ln[3:4])


def _post_attn_call(x2, o, w_o, w_gate, w_up, w_down, ln, *, tm):
    s, d = x2.shape
    d_ff = w_gate.shape[1]
    return pl.pallas_call(
        _post_attn_kernel,
        grid=(s // tm,),
        in_specs=[
            pl.BlockSpec((tm, d), lambda i: (i, 0)),
            pl.BlockSpec((tm, o.shape[1]), lambda i: (i, 0)),
            _resident(w_o.shape),
            _resident(w_gate.shape),
            _resident(w_up.shape),
            _resident(w_down.shape),
            _resident(ln.shape),
        ],
        out_specs=pl.BlockSpec((tm, d), lambda i: (i, 0)),
        out_shape=jax.ShapeDtypeStruct((s, d), F32),
        scratch_shapes=[pltpu.VMEM((tm, d_ff), BF16)],
        compiler_params=pltpu.CompilerParams(
            dimension_semantics=("arbitrary",), vmem_limit_bytes=V7X_VMEM_LIMIT_BYTES),
        name="attn_out_ffn",
    )(x2, o, w_o, w_gate, w_up, w_down, ln)


def _conv_layer_kernel(h_ref, win_ref, wc_ref, wout_ref, wg_ref, wu_ref, wd_ref, ln_ref, out_ref,
                       act_sc, tail_sc, *, tm, ch):
    i = pl.program_id(0)
    ln = ln_ref[...]
    h = h_ref[...]
    hb = h.astype(BF16)

    @pl.when(i == 0)
    def _():
        tail_sc[...] = jnp.zeros(tail_sc.shape, F32)

    b_gate = jnp.dot(hb, win_ref[:, 0:ch], preferred_element_type=F32)
    c_gate = jnp.dot(hb, win_ref[:, ch:2 * ch], preferred_element_type=F32)
    xt = jnp.dot(hb, win_ref[:, 2 * ch:3 * ch], preferred_element_type=F32)
    u = c_gate * xt

    row = lax.broadcasted_iota(jnp.int32, (tm, ch), 0)
    tail = tail_sc[...]
    wc = wc_ref[...]
    y = wc[CONV_WIDTH - 1:CONV_WIDTH] * u
    for d in range(1, CONV_WIDTH):
        shifted = pltpu.roll(u, d, 0)
        for t in range(d):
            src = CONV_WIDTH - 1 - d + t
            shifted = jnp.where(row == t, tail[src:src + 1], shifted)
        y = y + wc[CONV_WIDTH - 1 - d:CONV_WIDTH - d] * shifted
    tail_sc[...] = u[tm - (CONV_WIDTH - 1):]

    m = jnp.dot((b_gate * y).astype(BF16), wout_ref[...], preferred_element_type=F32)
    h1 = _layer_norm(ALPHA * h + m, ln[0:1], ln[1:2])
    out_ref[...] = _ffn_ln(h1, wg_ref, wu_ref, wd_ref, act_sc, ln[2:3], ln[3:4])


def _conv_layer_call(h, w_in, w_conv, w_out, w_gate, w_up, w_down, ln, *, tm):
    s, d = h.shape
    ch = w_out.shape[0]
    d_ff = w_gate.shape[1]
    return pl.pallas_call(
        functools.partial(_conv_layer_kernel, tm=tm, ch=ch),
        grid=(s // tm,),
        in_specs=[
            pl.BlockSpec((tm, d), lambda i: (i, 0)),
            _resident(w_in.shape),
            _resident(w_conv.shape),
            _resident(w_out.shape),
            _resident(w_gate.shape),
            _resident(w_up.shape),
            _resident(w_down.shape),
            _resident(ln.shape),
        ],
        out_specs=pl.BlockSpec((tm, d), lambda i: (i, 0)),
        out_shape=jax.ShapeDtypeStruct((s, d), F32),
        scratch_shapes=[pltpu.VMEM((tm, d_ff), BF16), pltpu.VMEM((CONV_WIDTH - 1, ch), F32)],
        compiler_params=pltpu.CompilerParams(
            dimension_semantics=("arbitrary",), vmem_limit_bytes=V7X_VMEM_LIMIT_BYTES),
        name="conv_ffn",
    )(h, w_in, w_conv, w_out, w_gate, w_up, w_down, ln)


def _rope_inv_lanes():
    inv = ROPE_THETA ** (-jnp.arange(0, ROPE_DIM, 2, dtype=F32) / ROPE_DIM)
    per_comp = jnp.concatenate([inv, inv, jnp.zeros((HEAD_DIM - ROPE_DIM,), F32)])
    return jnp.tile(per_comp, HEAD_W // HEAD_DIM)[None, :]


def kernel(x, attn_w_qkv, attn_w_o, attn_lambda_q1, attn_lambda_k1, attn_lambda_q2, attn_lambda_k2,
           attn_subln_g, conv_w_in, conv_w, conv_w_out, ffn_w_gate, ffn_w_up, ffn_w_down,
           ln1_g, ln1_b, ln2_g, ln2_b):
    b, s, d = x.shape
    assert b == 1 and attn_w_qkv.shape[0] == 1 and conv_w_in.shape[0] == 1
    tm = min(512, s)
    tq, tk = min(512, s), min(256, s)
    x2 = x.reshape(s, d)

    qt, k, vt = _qkv_call(x2, attn_w_qkv[0].astype(BF16), _rope_inv_lanes(), tm=tm)
    lam_p = jnp.concatenate([attn_lambda_q1, attn_lambda_k1, attn_lambda_q2, attn_lambda_k2], axis=0)
    o = _attn_call(lam_p, attn_subln_g, qt, k, vt, tq=tq, tk=tk, lam_init=LAMBDA_INIT_0)

    ln0 = jnp.stack([ln1_g[0], ln1_b[0], ln2_g[0], ln2_b[0]])
    h = _post_attn_call(x2, o, attn_w_o[0].astype(BF16), ffn_w_gate[0].astype(BF16),
                        ffn_w_up[0].astype(BF16), ffn_w_down[0].astype(BF16), ln0, tm=tm)

    ln1 = jnp.stack([ln1_g[1], ln1_b[1], ln2_g[1], ln2_b[1]])
    out = _conv_layer_call(h, conv_w_in[0].astype(BF16), conv_w[0], conv_w_out[0].astype(BF16),
                           ffn_w_gate[1].astype(BF16), ffn_w_up[1].astype(BF16),
                           ffn_w_down[1].astype(BF16), ln1, tm=tm)
    return out.reshape(b, s, d)
```

```python
import functools
import math

import jax
import jax.numpy as jnp
from jax import lax
from jax.experimental import pallas as pl
from jax.experimental.pallas import tpu as pltpu

F32 = jnp.float32
BF16 = jnp.bfloat16

CHUNK = 64
N_HEADS = 8
HEAD_DIM = 64
HEAD_W = 2 * HEAD_DIM
ROPE_DIM = HEAD_DIM // 4
ROPE_HALF = ROPE_DIM // 2
ROPE_THETA = 500000.0
CONV_WIDTH = 3
DEPTH = 2
ALPHA = (2 * DEPTH) ** 0.25
LN_EPS = 1e-5
NEG_INF = -1e30
LAMBDA_INIT_0 = 0.8 - 0.6 * math.exp(-0.3 * 0)

V7X_VMEM_LIMIT_BYTES = 56 * 1024 * 1024
MXU_WIDTH = 256
BF16_SUBLANES = 16
VT_ROWS = HEAD_W + BF16_SUBLANES
Q_SCALE = HEAD_DIM ** -0.5 * math.log2(math.e)


def _resident(shape):
    return pl.BlockSpec(shape, lambda *_: (0,) * len(shape), pipeline_mode=pl.Buffered(1))


def _layer_norm(x, g, b):
    mu = jnp.mean(x, axis=-1, keepdims=True)
    xc = x - mu
    var = jnp.mean(xc * xc, axis=-1, keepdims=True)
    return xc * lax.rsqrt(var + LN_EPS) * g + b


def _qkv_kernel(x_ref, w_ref, inv_ref, qt_ref, k_ref, vt_ref, *, tm, width):
    i = pl.program_id(0)
    xb = x_ref[...].astype(BF16)

    row = lax.broadcasted_iota(jnp.int32, (tm, HEAD_W), 0)
    lane = lax.broadcasted_iota(jnp.int32, (tm, HEAD_W), 1)
    w = lane % HEAD_DIM
    ang = (i * tm + row).astype(F32) * inv_ref[...]
    cos, sin = jnp.cos(ang), jnp.sin(ang)
    c_tab = jnp.where(w < ROPE_DIM, cos, 1.0)
    s_up = jnp.where(w < ROPE_HALF, -sin, 0.0)
    s_dn = jnp.where((w >= ROPE_HALF) & (w < ROPE_DIM), sin, 0.0)
    first_t = lax.broadcasted_iota(jnp.int32, (HEAD_W, tm), 0) < HEAD_DIM

    def rope(t):
        up = pltpu.roll(t, HEAD_W - ROPE_HALF, 1)
        dn = pltpu.roll(t, ROPE_HALF, 1)
        return t * c_tab + up * s_up + dn * s_dn

    def proj(col):
        return jnp.dot(xb, w_ref[:, col:col + MXU_WIDTH], preferred_element_type=F32)

    ones_row = (lax.broadcasted_iota(jnp.int32, (BF16_SUBLANES, tm), 0) == 0).astype(BF16)

    for c in range(width // MXU_WIDTH):
        q2 = proj(c * MXU_WIDTH)
        k2 = proj(width + c * MXU_WIDTH)
        v2 = proj(2 * width + c * MXU_WIDTH)
        for half in range(MXU_WIDTH // HEAD_W):
            head = c * (MXU_WIDTH // HEAD_W) + half
            lo = head * HEAD_W
            cols = slice(half * HEAD_W, (half + 1) * HEAD_W)
            qh_t = (rope(q2[:, cols]) * Q_SCALE).T
            qt_ref[0, lo:lo + HEAD_W, :] = jnp.where(first_t, qh_t, 0.0).astype(BF16)
            qt_ref[1, lo:lo + HEAD_W, :] = jnp.where(first_t, 0.0, qh_t).astype(BF16)
            k_ref[:, lo:lo + HEAD_W] = rope(k2[:, cols]).astype(BF16)
            vlo = head * VT_ROWS
            vt_ref[vlo:vlo + HEAD_W, :] = v2[:, cols].T.astype(BF16)
            vt_ref[vlo + HEAD_W:vlo + VT_ROWS, :] = ones_row


def _qkv_call(x2, w_qkv, inv_lane, *, tm):
    s, d = x2.shape
    width = w_qkv.shape[1] // 3
    return pl.pallas_call(
        functools.partial(_qkv_kernel, tm=tm, width=width),
        grid=(s // tm,),
        in_specs=[
            pl.BlockSpec((tm, d), lambda i: (i, 0)),
            _resident(w_qkv.shape),
            _resident(inv_lane.shape),
        ],
        out_specs=[
            pl.BlockSpec((2, width, tm), lambda i: (0, 0, i)),
            pl.BlockSpec((tm, width), lambda i: (i, 0)),
            pl.BlockSpec((N_HEADS * VT_ROWS, tm), lambda i: (0, i)),
        ],
        out_shape=[
            jax.ShapeDtypeStruct((2, width, s), BF16),
            jax.ShapeDtypeStruct((s, width), BF16),
            jax.ShapeDtypeStruct((N_HEADS * VT_ROWS, s), BF16),
        ],
        compiler_params=pltpu.CompilerParams(
            dimension_semantics=("arbitrary",), vmem_limit_bytes=V7X_VMEM_LIMIT_BYTES),
        name="qkv_rope",
    )(x2, w_qkv, inv_lane)


def _attn_kernel(lam_ref, g_ref, qt_ref, k_ref, vt_ref, o_ref, m_sc, acc_sc, s_sc, p_sc, a_sc, smax_sc,
                 *, tq, tk, lam_init):
    i = pl.program_id(1)

    m_sc[...] = jnp.full(m_sc.shape, NEG_INF, F32)
    acc_sc[...] = jnp.zeros(acc_sc.shape, F32)
    p_sc[...] = jnp.zeros(p_sc.shape, BF16)
    a_sc[...] = jnp.ones(a_sc.shape, F32)

    def scores(t):
        start = pl.multiple_of(t * tk, tk)
        k_t = k_ref[pl.ds(start, tk), :]
        for c in range(2):
            s = jnp.dot(k_t, qt_ref[c], preferred_element_type=F32)
            s_sc[c] = s
            smax_sc[c] = jnp.max(s, axis=0, keepdims=True)

    def softmax(mask):
        for c in range(2):
            s = s_sc[c]
            if mask is None:
                s_max = smax_sc[c]
            else:
                s = jnp.where(mask, s, NEG_INF)
                s_max = jnp.max(s, axis=0, keepdims=True)
            m_prev = m_sc[c]
            m_new = jnp.maximum(m_prev, s_max)
            p_sc[c] = jnp.exp2(s - m_new).astype(BF16)
            a_sc[c] = jnp.exp2(m_prev - m_new)
            m_sc[c] = m_new

    def accumulate(t):
        start = pl.multiple_of(t * tk, tk)
        v_t = vt_ref[:, pl.ds(start, tk)]
        for c in range(2):
            acc_sc[c] = a_sc[c] * acc_sc[c] + jnp.dot(v_t, p_sc[c], preferred_element_type=F32)

    n_sub = tq // tk
    n_full = i * n_sub

    scores(0)

    def step(t, carry):
        accumulate(jnp.maximum(t - 1, 0))
        softmax(None)
        scores(t + 1)
        return carry

    lax.fori_loop(0, n_full, step, 0)

    key = lax.broadcasted_iota(jnp.int32, (tk, tq), 0)
    qry = lax.broadcasted_iota(jnp.int32, (tk, tq), 1)
    for r in range(n_sub):
        accumulate(jnp.maximum(n_full + r - 1, 0))
        softmax((r * tk + key) // CHUNK <= qry // CHUNK)
        if r + 1 < n_sub:
            scores(n_full + r + 1)
    accumulate(n_full + n_sub - 1)

    lam_p = lam_ref[...]
    lam = (jnp.exp(jnp.sum(lam_p[0:1] * lam_p[1:2], axis=1, keepdims=True))
           - jnp.exp(jnp.sum(lam_p[2:3] * lam_p[3:4], axis=1, keepdims=True)) + lam_init)

    def normalised(c):
        return acc_sc[c, :HEAD_W] / acc_sc[c, HEAD_W:HEAD_W + 1]

    o = (normalised(0) - lam * normalised(1)).T
    o = o * lax.rsqrt(jnp.mean(o * o, axis=-1, keepdims=True) + LN_EPS) * g_ref[...]
    o_ref[...] = (o * (1.0 - lam_init)).astype(o_ref.dtype)


def _attn_call(lam_p, g_sub, qt, k, vt, *, tq, tk, lam_init):
    s, width = k.shape
    return pl.pallas_call(
        functools.partial(_attn_kernel, tq=tq, tk=tk, lam_init=lam_init),
        grid=(N_HEADS, s // tq),
        in_specs=[
            pl.BlockSpec(lam_p.shape, lambda h, i: (0, 0)),
            pl.BlockSpec(g_sub.shape, lambda h, i: (0, 0)),
            pl.BlockSpec((2, HEAD_W, tq), lambda h, i: (0, h, i)),
            pl.BlockSpec((s, HEAD_W), lambda h, i: (0, h)),
            pl.BlockSpec((VT_ROWS, s), lambda h, i: (h, 0)),
        ],
        out_specs=pl.BlockSpec((tq, HEAD_W), lambda h, i: (i, h)),
        out_shape=jax.ShapeDtypeStruct((s, width), BF16),
        scratch_shapes=[
            pltpu.VMEM((2, 1, tq), F32),
            pltpu.VMEM((2, VT_ROWS, tq), F32),
            pltpu.VMEM((2, tk, tq), F32),
            pltpu.VMEM((2, tk, tq), BF16),
            pltpu.VMEM((2, 1, tq), F32),
            pltpu.VMEM((2, 1, tq), F32),
        ],
        compiler_params=pltpu.CompilerParams(
            dimension_semantics=("arbitrary", "arbitrary"), vmem_limit_bytes=V7X_VMEM_LIMIT_BYTES),
        name="diff_attn",
    )(lam_p, g_sub, qt, k, vt)


def _ffn_ln(h, wg_ref, wu_ref, wd_ref, act_sc, g2, b2):
    hb = h.astype(BF16)
    d_ff = wg_ref.shape[1]
    for c in range(d_ff // MXU_WIDTH):
        cols = slice(c * MXU_WIDTH, (c + 1) * MXU_WIDTH)
        gate = jnp.dot(hb, wg_ref[:, cols], preferred_element_type=F32)
        up = jnp.dot(hb, wu_ref[:, cols], preferred_element_type=F32)
        act_sc[:, cols] = (gate / (1.0 + jnp.exp(-gate)) * up).astype(BF16)
    y = jnp.dot(act_sc[...], wd_ref[...], preferred_element_type=F32)
    return _layer_norm(ALPHA * h + y, g2, b2)


def _post_attn_kernel(x_ref, o_ref, wo_ref, wg_ref, wu_ref, wd_ref, ln_ref, out_ref, act_sc):
    ln = ln_ref[...]
    m = jnp.dot(o_ref[...], wo_ref[...], preferred_element_type=F32)
    h = _layer_norm(ALPHA * x_ref[...] + m, ln[0:1], ln[1:2])
    out_ref[...] = _ffn_ln(h, wg_ref, wu_ref, wd_ref, act_sc, ln[2:3], ln[3:4])


def _post_attn_call(x2, o, w_o, w_gate, w_up, w_down, ln, *, tm):
    s, d = x2.shape
    d_ff = w_gate.shape[1]
    return pl.pallas_call(
        _post_attn_kernel,
        grid=(s // tm,),
        in_specs=[
            pl.BlockSpec((tm, d), lambda i: (i, 0)),
            pl.BlockSpec((tm, o.shape[1]), lambda i: (i, 0)),
            _resident(w_o.shape),
            _resident(w_gate.shape),
            _resident(w_up.shape),
            _resident(w_down.shape),
            _resident(ln.shape),
        ],
        out_specs=pl.BlockSpec((tm, d), lambda i: (i, 0)),
        out_shape=jax.ShapeDtypeStruct((s, d), F32),
        scratch_shapes=[pltpu.VMEM((tm, d_ff), BF16)],
        compiler_params=pltpu.CompilerParams(
            dimension_semantics=("arbitrary",), vmem_limit_bytes=V7X_VMEM_LIMIT_BYTES),
        name="attn_out_ffn",
    )(x2, o, w_o, w_gate, w_up, w_down, ln)


def _conv_layer_kernel(h_ref, win_ref, wc_ref, wout_ref, wg_ref, wu_ref, wd_ref, ln_ref, out_ref,
                       act_sc, tail_sc, *, tm, ch):
    i = pl.program_id(0)
    ln = ln_ref[...]
    h = h_ref[...]
    hb = h.astype(BF16)

    @pl.when(i == 0)
    def _():
        tail_sc[...] = jnp.zeros(tail_sc.shape, F32)

    b_gate = jnp.dot(hb, win_ref[:, 0:ch], preferred_element_type=F32)
    c_gate = jnp.dot(hb, win_ref[:, ch:2 * ch], preferred_element_type=F32)
    xt = jnp.dot(hb, win_ref[:, 2 * ch:3 * ch], preferred_element_type=F32)
    u = c_gate * xt

    row = lax.broadcasted_iota(jnp.int32, (tm, ch), 0)
    tail = tail_sc[...]
    wc = wc_ref[...]
    y = wc[CONV_WIDTH - 1:CONV_WIDTH] * u
    for d in range(1, CONV_WIDTH):
        shifted = pltpu.roll(u, d, 0)
        for t in range(d):
            src = CONV_WIDTH - 1 - d + t
            shifted = jnp.where(row == t, tail[src:src + 1], shifted)
        y = y + wc[CONV_WIDTH - 1 - d:CONV_WIDTH - d] * shifted
    tail_sc[...] = u[tm - (CONV_WIDTH - 1):]

    m = jnp.dot((b_gate * y).astype(BF16), wout_ref[...], preferred_element_type=F32)
    h1 = _layer_norm(ALPHA * h + m, ln[0:1], ln[1:2])
    out_ref[...] = _ffn_ln(h1, wg_ref, wu_ref, wd_ref, act_sc, ln[2:3], ln[3:4])


def _conv_layer_call(h, w_in, w_conv, w_out, w_gate, w_up, w_down, ln, *, tm):
    s, d = h.shape
    ch = w_out.shape[0]
    d_ff = w_gate.shape[1]
    return pl.pallas_call(
        functools.partial(_conv_layer_kernel, tm=tm, ch=ch),
        grid=(s // tm,),
        in_specs=[
            pl.BlockSpec((tm, d), lambda i: (i, 0)),
            _resident(w_in.shape),
            _resident(w_conv.shape),
            _resident(w_out.shape),
            _resident(w_gate.shape),
            _resident(w_up.shape),
            _resident(w_down.shape),
            _resident(ln.shape),
        ],
        out_specs=pl.BlockSpec((tm, d), lambda i: (i, 0)),
        out_shape=jax.ShapeDtypeStruct((s, d), F32),
        scratch_shapes=[pltpu.VMEM((tm, d_ff), BF16), pltpu.VMEM((CONV_WIDTH - 1, ch), F32)],
        compiler_params=pltpu.CompilerParams(
            dimension_semantics=("arbitrary",), vmem_limit_bytes=V7X_VMEM_LIMIT_BYTES),
        name="conv_ffn",
    )(h, w_in, w_conv, w_out, w_gate, w_up, w_down, ln)


def _rope_inv_lanes():
    inv = ROPE_THETA ** (-jnp.arange(0, ROPE_DIM, 2, dtype=F32) / ROPE_DIM)
    per_comp = jnp.concatenate([inv, inv, jnp.zeros((HEAD_DIM - ROPE_DIM,), F32)])
    return jnp.tile(per_comp, HEAD_W // HEAD_DIM)[None, :]


def kernel(x, attn_w_qkv, attn_w_o, attn_lambda_q1, attn_lambda_k1, attn_lambda_q2, attn_lambda_k2,
           attn_subln_g, conv_w_in, conv_w, conv_w_out, ffn_w_gate, ffn_w_up, ffn_w_down,
           ln1_g, ln1_b, ln2_g, ln2_b):
    b, s, d = x.shape
    assert b == 1 and attn_w_qkv.shape[0] == 1 and conv_w_in.shape[0] == 1
    tm = min(512, s)
    tq, tk = min(1024, s), min(512, s)
    x2 = x.reshape(s, d)

    qt, k, vt = _qkv_call(x2, attn_w_qkv[0].astype(BF16), _rope_inv_lanes(), tm=tm)
    lam_p = jnp.concatenate([attn_lambda_q1, attn_lambda_k1, attn_lambda_q2, attn_lambda_k2], axis=0)
    o = _attn_call(lam_p, attn_subln_g, qt, k, vt, tq=tq, tk=tk, lam_init=LAMBDA_INIT_0)

    ln0 = jnp.stack([ln1_g[0], ln1_b[0], ln2_g[0], ln2_b[0]])
    h = _post_attn_call(x2, o, attn_w_o[0].astype(BF16), ffn_w_gate[0].astype(BF16),
                        ffn_w_up[0].astype(BF16), ffn_w_down[0].astype(BF16), ln0, tm=tm)

    ln1 = jnp.stack([ln1_g[1], ln1_b[1], ln2_g[1], ln2_b[1]])
    out = _conv_layer_call(h, conv_w_in[0].astype(BF16), conv_w[0], conv_w_out[0].astype(BF16),
                           ffn_w_gate[1].astype(BF16), ffn_w_up[1].astype(BF16),
                           ffn_w_down[1].astype(BF16), ln1, tm=tm)
    return out.reshape(b, s, d)
```

```python
import functools
import math

import jax
import jax.numpy as jnp
from jax import lax
from jax.experimental import pallas as pl
from jax.experimental.pallas import tpu as pltpu

F32 = jnp.float32
BF16 = jnp.bfloat16

CHUNK = 64
N_HEADS = 8
HEAD_DIM = 64
HEAD_W = 2 * HEAD_DIM
ROPE_DIM = HEAD_DIM // 4
ROPE_HALF = ROPE_DIM // 2
ROPE_THETA = 500000.0
CONV_WIDTH = 3
DEPTH = 2
ALPHA = (2 * DEPTH) ** 0.25
LN_EPS = 1e-5
NEG_INF = -1e30
LAMBDA_INIT_0 = 0.8 - 0.6 * math.exp(-0.3 * 0)

V7X_VMEM_LIMIT_BYTES = 56 * 1024 * 1024
MXU_WIDTH = 256
BF16_SUBLANES = 16
VT_ROWS = HEAD_W + BF16_SUBLANES
Q_SCALE = HEAD_DIM ** -0.5 * math.log2(math.e)


def _resident(shape):
    return pl.BlockSpec(shape, lambda *_: (0,) * len(shape), pipeline_mode=pl.Buffered(1))


def _layer_norm(x, g, b):
    mu = jnp.mean(x, axis=-1, keepdims=True)
    xc = x - mu
    var = jnp.mean(xc * xc, axis=-1, keepdims=True)
    return xc * lax.rsqrt(var + LN_EPS) * g + b


def _qkv_kernel(x_ref, w_ref, inv_ref, qt_ref, k_ref, vt_ref, *, tm, width):
    i = pl.program_id(0)
    xb = x_ref[...].astype(BF16)

    row = lax.broadcasted_iota(jnp.int32, (tm, HEAD_W), 0)
    lane = lax.broadcasted_iota(jnp.int32, (tm, HEAD_W), 1)
    w = lane % HEAD_DIM
    ang = (i * tm + row).astype(F32) * inv_ref[...]
    cos, sin = jnp.cos(ang), jnp.sin(ang)
    c_tab = jnp.where(w < ROPE_DIM, cos, 1.0)
    s_up = jnp.where(w < ROPE_HALF, -sin, 0.0)
    s_dn = jnp.where((w >= ROPE_HALF) & (w < ROPE_DIM), sin, 0.0)
    first_t = lax.broadcasted_iota(jnp.int32, (HEAD_W, tm), 0) < HEAD_DIM

    def rope(t):
        up = pltpu.roll(t, HEAD_W - ROPE_HALF, 1)
        dn = pltpu.roll(t, ROPE_HALF, 1)
        return t * c_tab + up * s_up + dn * s_dn

    def proj(col):
        return jnp.dot(xb, w_ref[:, col:col + MXU_WIDTH], preferred_element_type=F32)

    ones_row = (lax.broadcasted_iota(jnp.int32, (BF16_SUBLANES, tm), 0) == 0).astype(BF16)

    for c in range(width // MXU_WIDTH):
        q2 = proj(c * MXU_WIDTH)
        k2 = proj(width + c * MXU_WIDTH)
        v2 = proj(2 * width + c * MXU_WIDTH)
        for half in range(MXU_WIDTH // HEAD_W):
            head = c * (MXU_WIDTH // HEAD_W) + half
            lo = head * HEAD_W
            cols = slice(half * HEAD_W, (half + 1) * HEAD_W)
            qh_t = (rope(q2[:, cols]) * Q_SCALE).T
            qt_ref[0, lo:lo + HEAD_W, :] = jnp.where(first_t, qh_t, 0.0).astype(BF16)
            qt_ref[1, lo:lo + HEAD_W, :] = jnp.where(first_t, 0.0, qh_t).astype(BF16)
            k_ref[:, lo:lo + HEAD_W] = rope(k2[:, cols]).astype(BF16)
            vlo = head * VT_ROWS
            vt_ref[vlo:vlo + HEAD_W, :] = v2[:, cols].T.astype(BF16)
            vt_ref[vlo + HEAD_W:vlo + VT_ROWS, :] = ones_row


def _qkv_call(x2, w_qkv, inv_lane, *, tm):
    s, d = x2.shape
    width = w_qkv.shape[1] // 3
    return pl.pallas_call(
        functools.partial(_qkv_kernel, tm=tm, width=width),
        grid=(s // tm,),
        in_specs=[
            pl.BlockSpec((tm, d), lambda i: (i, 0)),
            _resident(w_qkv.shape),
            _resident(inv_lane.shape),
        ],
        out_specs=[
            pl.BlockSpec((2, width, tm), lambda i: (0, 0, i)),
            pl.BlockSpec((tm, width), lambda i: (i, 0)),
            pl.BlockSpec((N_HEADS * VT_ROWS, tm), lambda i: (0, i)),
        ],
        out_shape=[
            jax.ShapeDtypeStruct((2, width, s), BF16),
            jax.ShapeDtypeStruct((s, width), BF16),
            jax.ShapeDtypeStruct((N_HEADS * VT_ROWS, s), BF16),
        ],
        compiler_params=pltpu.CompilerParams(
            dimension_semantics=("arbitrary",), vmem_limit_bytes=V7X_VMEM_LIMIT_BYTES),
        name="qkv_rope",
    )(x2, w_qkv, inv_lane)


def _attn_kernel(lam_ref, g_ref, qt_ref, k_ref, vt_ref, o_ref, m_sc, acc_sc, s_sc, p_sc, a_sc, smax_sc,
                 *, tile, n_blocks, lam_init):
    n_pairs = n_blocks * (n_blocks + 1) // 2

    m_sc[...] = jnp.full(m_sc.shape, NEG_INF, F32)
    acc_sc[...] = jnp.zeros(acc_sc.shape, F32)
    p_sc[...] = jnp.zeros(p_sc.shape, BF16)
    a_sc[...] = jnp.ones(a_sc.shape, F32)

    key = lax.broadcasted_iota(jnp.int32, (tile, tile), 0)
    qry = lax.broadcasted_iota(jnp.int32, (tile, tile), 1)
    diag_mask = key // CHUNK <= qry // CHUNK

    def scores(i, t):
        k_t = k_ref[pl.ds(pl.multiple_of(t * tile, tile), tile), :]
        q_lo = pl.multiple_of(i * tile, tile)
        for c in range(2):
            s = jnp.dot(k_t, qt_ref[c, :, pl.ds(q_lo, tile)], preferred_element_type=F32)
            s_sc[c] = s
            smax_sc[c] = jnp.max(s, axis=0, keepdims=True)

    def softmax(t, masked):
        for c in range(2):
            s = s_sc[c]
            if masked:
                s = jnp.where(diag_mask, s, NEG_INF)
                s_max = jnp.max(s, axis=0, keepdims=True)
            else:
                s_max = smax_sc[c]
            m_prev = jnp.where(t == 0, NEG_INF, m_sc[c])
            m_new = jnp.maximum(m_prev, s_max)
            p_sc[c] = jnp.exp2(s - m_new).astype(BF16)
            a_sc[c] = jnp.exp2(m_prev - m_new)
            m_sc[c] = m_new

    def accumulate(t):
        v_t = vt_ref[:, pl.ds(pl.multiple_of(t * tile, tile), tile)]
        for c in range(2):
            acc_sc[c] = a_sc[c] * acc_sc[c] + jnp.dot(v_t, p_sc[c], preferred_element_type=F32)

    def finalize(i):
        lam_p = lam_ref[...]
        lam = (jnp.exp(jnp.sum(lam_p[0:1] * lam_p[1:2], axis=1, keepdims=True))
               - jnp.exp(jnp.sum(lam_p[2:3] * lam_p[3:4], axis=1, keepdims=True)) + lam_init)

        def normalised(c):
            return acc_sc[c, :HEAD_W] / acc_sc[c, HEAD_W:HEAD_W + 1]

        o = (normalised(0) - lam * normalised(1)).T
        o = o * lax.rsqrt(jnp.mean(o * o, axis=-1, keepdims=True) + LN_EPS) * g_ref[...]
        o_ref[pl.ds(pl.multiple_of(i * tile, tile), tile), :] = (o * (1.0 - lam_init)).astype(o_ref.dtype)

    def following(i, t):
        last = t == i
        return jnp.where(last, i + 1, i), jnp.where(last, 0, t + 1)

    scores(0, 0)

    def step(n, carry):
        i3, t3, i2, t2, i1, t1 = carry

        @pl.when((n >= 2) & (t3 == 0))
        def _():
            finalize(i3 - 1)

        def stages(masked):
            accumulate(t3)
            softmax(t2, masked)
            scores(jnp.minimum(i1, n_blocks - 1), jnp.minimum(t1, n_blocks - 1))

        lax.cond(t2 == i2, lambda: stages(True), lambda: stages(False))
        return (i2, t2, i1, t1) + following(i1, t1)

    zero = jnp.int32(0)
    lax.fori_loop(0, n_pairs + 1, step, (zero, zero, zero, zero) + following(zero, zero))
    finalize(n_blocks - 1)


def _attn_call(lam_p, g_sub, qt, k, vt, *, tile, lam_init):
    s, width = k.shape
    return pl.pallas_call(
        functools.partial(_attn_kernel, tile=tile, n_blocks=s // tile, lam_init=lam_init),
        grid=(N_HEADS,),
        in_specs=[
            pl.BlockSpec(lam_p.shape, lambda h: (0, 0)),
            pl.BlockSpec(g_sub.shape, lambda h: (0, 0)),
            pl.BlockSpec((2, HEAD_W, s), lambda h: (0, h, 0), pipeline_mode=pl.Buffered(1)),
            pl.BlockSpec((s, HEAD_W), lambda h: (0, h)),
            pl.BlockSpec((VT_ROWS, s), lambda h: (h, 0)),
        ],
        out_specs=pl.BlockSpec((s, HEAD_W), lambda h: (0, h)),
        out_shape=jax.ShapeDtypeStruct((s, width), BF16),
        scratch_shapes=[
            pltpu.VMEM((2, 1, tile), F32),
            pltpu.VMEM((2, VT_ROWS, tile), F32),
            pltpu.VMEM((2, tile, tile), F32),
            pltpu.VMEM((2, tile, tile), BF16),
            pltpu.VMEM((2, 1, tile), F32),
            pltpu.VMEM((2, 1, tile), F32),
        ],
        compiler_params=pltpu.CompilerParams(
            dimension_semantics=("arbitrary",), vmem_limit_bytes=V7X_VMEM_LIMIT_BYTES),
        name="diff_attn",
    )(lam_p, g_sub, qt, k, vt)


def _ffn_ln(h, wg_ref, wu_ref, wd_ref, act_sc, g2, b2):
    hb = h.astype(BF16)
    d_ff = wg_ref.shape[1]
    for c in range(d_ff // MXU_WIDTH):
        cols = slice(c * MXU_WIDTH, (c + 1) * MXU_WIDTH)
        gate = jnp.dot(hb, wg_ref[:, cols], preferred_element_type=F32)
        up = jnp.dot(hb, wu_ref[:, cols], preferred_element_type=F32)
        act_sc[:, cols] = (gate / (1.0 + jnp.exp(-gate)) * up).astype(BF16)
    y = jnp.dot(act_sc[...], wd_ref[...], preferred_element_type=F32)
    return _layer_norm(ALPHA * h + y, g2, b2)


def _post_attn_kernel(x_ref, o_ref, wo_ref, wg_ref, wu_ref, wd_ref, ln_ref, out_ref, act_sc):
    ln = ln_ref[...]
    m = jnp.dot(o_ref[...], wo_ref[...], preferred_element_type=F32)
    h = _layer_norm(ALPHA * x_ref[...] + m, ln[0:1], ln[1:2])
    out_ref[...] = _ffn_ln(h, wg_ref, wu_ref, wd_ref, act_sc, ln[2:3], ln[3:4])


def _post_attn_call(x2, o, w_o, w_gate, w_up, w_down, ln, *, tm):
    s, d = x2.shape
    d_ff = w_gate.shape[1]
    return pl.pallas_call(
        _post_attn_kernel,
        grid=(s // tm,),
        in_specs=[
            pl.BlockSpec((tm, d), lambda i: (i, 0)),
            pl.BlockSpec((tm, o.shape[1]), lambda i: (i, 0)),
            _resident(w_o.shape),
            _resident(w_gate.shape),
            _resident(w_up.shape),
            _resident(w_down.shape),
            _resident(ln.shape),
        ],
        out_specs=pl.BlockSpec((tm, d), lambda i: (i, 0)),
        out_shape=jax.ShapeDtypeStruct((s, d), F32),
        scratch_shapes=[pltpu.VMEM((tm, d_ff), BF16)],
        compiler_params=pltpu.CompilerParams(
            dimension_semantics=("arbitrary",), vmem_limit_bytes=V7X_VMEM_LIMIT_BYTES),
        name="attn_out_ffn",
    )(x2, o, w_o, w_gate, w_up, w_down, ln)


def _conv_layer_kernel(h_ref, win_ref, wc_ref, wout_ref, wg_ref, wu_ref, wd_ref, ln_ref, out_ref,
                       act_sc, tail_sc, *, tm, ch):
    i = pl.program_id(0)
    ln = ln_ref[...]
    h = h_ref[...]
    hb = h.astype(BF16)

    @pl.when(i == 0)
    def _():
        tail_sc[...] = jnp.zeros(tail_sc.shape, F32)

    b_gate = jnp.dot(hb, win_ref[:, 0:ch], preferred_element_type=F32)
    c_gate = jnp.dot(hb, win_ref[:, ch:2 * ch], preferred_element_type=F32)
    xt = jnp.dot(hb, win_ref[:, 2 * ch:3 * ch], preferred_element_type=F32)
    u = c_gate * xt

    row = lax.broadcasted_iota(jnp.int32, (tm, ch), 0)
    tail = tail_sc[...]
    wc = wc_ref[...]
    y = wc[CONV_WIDTH - 1:CONV_WIDTH] * u
    for d in range(1, CONV_WIDTH):
        shifted = pltpu.roll(u, d, 0)
        for t in range(d):
            src = CONV_WIDTH - 1 - d + t
            shifted = jnp.where(row == t, tail[src:src + 1], shifted)
        y = y + wc[CONV_WIDTH - 1 - d:CONV_WIDTH - d] * shifted
    tail_sc[...] = u[tm - (CONV_WIDTH - 1):]

    m = jnp.dot((b_gate * y).astype(BF16), wout_ref[...], preferred_element_type=F32)
    h1 = _layer_norm(ALPHA * h + m, ln[0:1], ln[1:2])
    out_ref[...] = _ffn_ln(h1, wg_ref, wu_ref, wd_ref, act_sc, ln[2:3], ln[3:4])


def _conv_layer_call(h, w_in, w_conv, w_out, w_gate, w_up, w_down, ln, *, tm):
    s, d = h.shape
    ch = w_out.shape[0]
    d_ff = w_gate.shape[1]
    return pl.pallas_call(
        functools.partial(_conv_layer_kernel, tm=tm, ch=ch),
        grid=(s // tm,),
        in_specs=[
            pl.BlockSpec((tm, d), lambda i: (i, 0)),
            _resident(w_in.shape),
            _resident(w_conv.shape),
            _resident(w_out.shape),
            _resident(w_gate.shape),
            _resident(w_up.shape),
            _resident(w_down.shape),
            _resident(ln.shape),
        ],
        out_specs=pl.BlockSpec((tm, d), lambda i: (i, 0)),
        out_shape=jax.ShapeDtypeStruct((s, d), F32),
        scratch_shapes=[pltpu.VMEM((tm, d_ff), BF16), pltpu.VMEM((CONV_WIDTH - 1, ch), F32)],
        compiler_params=pltpu.CompilerParams(
            dimension_semantics=("arbitrary",), vmem_limit_bytes=V7X_VMEM_LIMIT_BYTES),
        name="conv_ffn",
    )(h, w_in, w_conv, w_out, w_gate, w_up, w_down, ln)


def _rope_inv_lanes():
    inv = ROPE_THETA ** (-jnp.arange(0, ROPE_DIM, 2, dtype=F32) / ROPE_DIM)
    per_comp = jnp.concatenate([inv, inv, jnp.zeros((HEAD_DIM - ROPE_DIM,), F32)])
    return jnp.tile(per_comp, HEAD_W // HEAD_DIM)[None, :]


def kernel(x, attn_w_qkv, attn_w_o, attn_lambda_q1, attn_lambda_k1, attn_lambda_q2, attn_lambda_k2,
           attn_subln_g, conv_w_in, conv_w, conv_w_out, ffn_w_gate, ffn_w_up, ffn_w_down,
           ln1_g, ln1_b, ln2_g, ln2_b):
    b, s, d = x.shape
    assert b == 1 and attn_w_qkv.shape[0] == 1 and conv_w_in.shape[0] == 1
    tm = min(512, s)
    tile = min(1024, s)
    x2 = x.reshape(s, d)

    qt, k, vt = _qkv_call(x2, attn_w_qkv[0].astype(BF16), _rope_inv_lanes(), tm=tm)
    lam_p = jnp.concatenate([attn_lambda_q1, attn_lambda_k1, attn_lambda_q2, attn_lambda_k2], axis=0)
    o = _attn_call(lam_p, attn_subln_g, qt, k, vt, tile=tile, lam_init=LAMBDA_INIT_0)

    ln0 = jnp.stack([ln1_g[0], ln1_b[0], ln2_g[0], ln2_b[0]])
    h = _post_attn_call(x2, o, attn_w_o[0].astype(BF16), ffn_w_gate[0].astype(BF16),
                        ffn_w_up[0].astype(BF16), ffn_w_down[0].astype(BF16), ln0, tm=tm)

    ln1 = jnp.stack([ln1_g[1], ln1_b[1], ln2_g[1], ln2_b[1]])
    out = _conv_layer_call(h, conv_w_in[0].astype(BF16), conv_w[0], conv_w_out[0].astype(BF16),
                           ffn_w_gate[1].astype(BF16), ffn_w_up[1].astype(BF16),
                           ffn_w_down[1].astype(BF16), ln1, tm=tm)
    return out.reshape(b, s, d)
```

```python
import functools
import math

import jax
import jax.numpy as jnp
from jax import lax
from jax.experimental import pallas as pl
from jax.experimental.pallas import tpu as pltpu

F32 = jnp.float32
BF16 = jnp.bfloat16

CHUNK = 64
N_HEADS = 8
HEAD_DIM = 64
HEAD_W = 2 * HEAD_DIM
ROPE_DIM = HEAD_DIM // 4
ROPE_HALF = ROPE_DIM // 2
ROPE_THETA = 500000.0
CONV_WIDTH = 3
DEPTH = 2
ALPHA = (2 * DEPTH) ** 0.25
LN_EPS = 1e-5
NEG_INF = -1e30
LAMBDA_INIT_0 = 0.8 - 0.6 * math.exp(-0.3 * 0)

V7X_VMEM_LIMIT_BYTES = 56 * 1024 * 1024
MXU_WIDTH = 256
BF16_SUBLANES = 16
VT_ROWS = HEAD_W + BF16_SUBLANES
Q_SCALE = HEAD_DIM ** -0.5 * math.log2(math.e)


def _resident(shape):
    return pl.BlockSpec(shape, lambda *_: (0,) * len(shape), pipeline_mode=pl.Buffered(1))


def _layer_norm(x, g, b):
    mu = jnp.mean(x, axis=-1, keepdims=True)
    xc = x - mu
    var = jnp.mean(xc * xc, axis=-1, keepdims=True)
    return xc * lax.rsqrt(var + LN_EPS) * g + b


def _qkv_kernel(x_ref, w_ref, inv_ref, qt_ref, k_ref, vt_ref, *, tm, width):
    i = pl.program_id(0)
    xb = x_ref[...].astype(BF16)

    row = lax.broadcasted_iota(jnp.int32, (tm, HEAD_W), 0)
    lane = lax.broadcasted_iota(jnp.int32, (tm, HEAD_W), 1)
    w = lane % HEAD_DIM
    ang = (i * tm + row).astype(F32) * inv_ref[...]
    cos, sin = jnp.cos(ang), jnp.sin(ang)
    c_tab = jnp.where(w < ROPE_DIM, cos, 1.0)
    s_up = jnp.where(w < ROPE_HALF, -sin, 0.0)
    s_dn = jnp.where((w >= ROPE_HALF) & (w < ROPE_DIM), sin, 0.0)
    first_t = lax.broadcasted_iota(jnp.int32, (HEAD_W, tm), 0) < HEAD_DIM

    def rope(t):
        up = pltpu.roll(t, HEAD_W - ROPE_HALF, 1)
        dn = pltpu.roll(t, ROPE_HALF, 1)
        return t * c_tab + up * s_up + dn * s_dn

    def proj(col):
        return jnp.dot(xb, w_ref[:, col:col + MXU_WIDTH], preferred_element_type=F32)

    ones_row = (lax.broadcasted_iota(jnp.int32, (BF16_SUBLANES, tm), 0) == 0).astype(BF16)

    for c in range(width // MXU_WIDTH):
        q2 = proj(c * MXU_WIDTH)
        k2 = proj(width + c * MXU_WIDTH)
        v2 = proj(2 * width + c * MXU_WIDTH)
        for half in range(MXU_WIDTH // HEAD_W):
            head = c * (MXU_WIDTH // HEAD_W) + half
            lo = head * HEAD_W
            cols = slice(half * HEAD_W, (half + 1) * HEAD_W)
            qh_t = (rope(q2[:, cols]) * Q_SCALE).T
            qt_ref[0, lo:lo + HEAD_W, :] = jnp.where(first_t, qh_t, 0.0).astype(BF16)
            qt_ref[1, lo:lo + HEAD_W, :] = jnp.where(first_t, 0.0, qh_t).astype(BF16)
            k_ref[:, lo:lo + HEAD_W] = rope(k2[:, cols]).astype(BF16)
            vlo = head * VT_ROWS
            vt_ref[vlo:vlo + HEAD_W, :] = v2[:, cols].T.astype(BF16)
            vt_ref[vlo + HEAD_W:vlo + VT_ROWS, :] = ones_row


def _qkv_call(x2, w_qkv, inv_lane, *, tm):
    s, d = x2.shape
    width = w_qkv.shape[1] // 3
    return pl.pallas_call(
        functools.partial(_qkv_kernel, tm=tm, width=width),
        grid=(s // tm,),
        in_specs=[
            pl.BlockSpec((tm, d), lambda i: (i, 0)),
            _resident(w_qkv.shape),
            _resident(inv_lane.shape),
        ],
        out_specs=[
            pl.BlockSpec((2, width, tm), lambda i: (0, 0, i)),
            pl.BlockSpec((tm, width), lambda i: (i, 0)),
            pl.BlockSpec((N_HEADS * VT_ROWS, tm), lambda i: (0, i)),
        ],
        out_shape=[
            jax.ShapeDtypeStruct((2, width, s), BF16),
            jax.ShapeDtypeStruct((s, width), BF16),
            jax.ShapeDtypeStruct((N_HEADS * VT_ROWS, s), BF16),
        ],
        compiler_params=pltpu.CompilerParams(
            dimension_semantics=("arbitrary",), vmem_limit_bytes=V7X_VMEM_LIMIT_BYTES),
        name="qkv_rope",
    )(x2, w_qkv, inv_lane)


def _attn_kernel(lam_ref, g_ref, qt_ref, k_ref, vt_ref, o_ref, m_sc, acc_sc, s_sc, p_sc, a_sc, smax_sc,
                 *, tile, n_blocks, lam_init):
    n_pairs = n_blocks * (n_blocks + 1) // 2
    n_steps = n_pairs // 2 + 1

    m_sc[...] = jnp.full(m_sc.shape, NEG_INF, F32)
    acc_sc[...] = jnp.zeros(acc_sc.shape, F32)
    p_sc[...] = jnp.zeros(p_sc.shape, BF16)
    a_sc[...] = jnp.ones(a_sc.shape, F32)

    key = lax.broadcasted_iota(jnp.int32, (tile, tile), 0)
    qry = lax.broadcasted_iota(jnp.int32, (tile, tile), 1)
    diag_mask = key // CHUNK <= qry // CHUNK

    def clamped(x):
        return jnp.minimum(x, n_blocks - 1)

    def scores(i, t):
        k_t = k_ref[pl.ds(pl.multiple_of(t * tile, tile), tile), :]
        q_lo = pl.multiple_of(i * tile, tile)
        for c in range(2):
            s = jnp.dot(k_t, qt_ref[c, :, pl.ds(q_lo, tile)], preferred_element_type=F32)
            s_sc[c] = s
            smax_sc[c] = jnp.max(s, axis=0, keepdims=True)

    def softmax(slot, t, masked):
        for c in range(2):
            s = s_sc[c]
            if masked:
                s = jnp.where(diag_mask, s, NEG_INF)
                s_max = jnp.max(s, axis=0, keepdims=True)
            else:
                s_max = smax_sc[c]
            m_prev = jnp.where(t == 0, NEG_INF, m_sc[c])
            m_new = jnp.maximum(m_prev, s_max)
            p_sc[slot, c] = jnp.exp2(s - m_new).astype(BF16)
            a_sc[slot, c] = jnp.exp2(m_prev - m_new)
            m_sc[c] = m_new

    def accumulate(slot, i, t):
        v_t = vt_ref[:, pl.ds(pl.multiple_of(clamped(t) * tile, tile), tile)]
        par = i % 2
        for c in range(2):
            acc_sc[par, c] = a_sc[slot, c] * acc_sc[par, c] + jnp.dot(
                v_t, p_sc[slot, c], preferred_element_type=F32)

    def finalize(i):
        lam_p = lam_ref[...]
        lam = (jnp.exp(jnp.sum(lam_p[0:1] * lam_p[1:2], axis=1, keepdims=True))
               - jnp.exp(jnp.sum(lam_p[2:3] * lam_p[3:4], axis=1, keepdims=True)) + lam_init)
        par = i % 2

        def normalised(c):
            return acc_sc[par, c, :HEAD_W] / acc_sc[par, c, HEAD_W:HEAD_W + 1]

        o = (normalised(0) - lam * normalised(1)).T
        o = o * lax.rsqrt(jnp.mean(o * o, axis=-1, keepdims=True) + LN_EPS) * g_ref[...]
        o_ref[pl.ds(pl.multiple_of(i * tile, tile), tile), :] = (o * (1.0 - lam_init)).astype(o_ref.dtype)

    def following(i, t):
        last = t == i
        return jnp.where(last, i + 1, i), jnp.where(last, 0, t + 1)

    scores(0, 0)

    def step(_, carry):
        i3, t3, i2, t2, i1, t1, i0, t0, done = carry

        @pl.when(done >= 0)
        def _():
            finalize(done)

        def stages(diag_a, diag_b):
            accumulate(1, i3, t3)
            softmax(0, t2, diag_a)
            scores(clamped(i1), clamped(t1))
            accumulate(0, i2, t2)
            softmax(1, t1, diag_b)
            scores(clamped(i0), clamped(t0))

        variant = jnp.where(t2 == i2, 1, jnp.where(t1 == i1, 2, 0))
        lax.switch(variant, [lambda: stages(False, False), lambda: stages(True, False),
                             lambda: stages(False, True)])
        done = jnp.where(t3 == i3, i3, jnp.where(t2 == i2, i2, -1))
        done = jnp.where(done < n_blocks, done, -1)
        nxt = following(i0, t0)
        return (i1, t1, i0, t0) + nxt + following(*nxt) + (done,)

    zero = jnp.int32(0)
    first = (zero, zero)
    second = following(*first)
    carry = lax.fori_loop(0, n_steps, step, first + first + second + following(*second) + (jnp.int32(-1),))

    @pl.when(carry[-1] >= 0)
    def _():
        finalize(carry[-1])


def _attn_call(lam_p, g_sub, qt, k, vt, *, tile, lam_init):
    s, width = k.shape
    return pl.pallas_call(
        functools.partial(_attn_kernel, tile=tile, n_blocks=s // tile, lam_init=lam_init),
        grid=(N_HEADS,),
        in_specs=[
            pl.BlockSpec(lam_p.shape, lambda h: (0, 0)),
            pl.BlockSpec(g_sub.shape, lambda h: (0, 0)),
            pl.BlockSpec((2, HEAD_W, s), lambda h: (0, h, 0), pipeline_mode=pl.Buffered(1)),
            pl.BlockSpec((s, HEAD_W), lambda h: (0, h)),
            pl.BlockSpec((VT_ROWS, s), lambda h: (h, 0), pipeline_mode=pl.Buffered(1)),
        ],
        out_specs=pl.BlockSpec((s, HEAD_W), lambda h: (0, h)),
        out_shape=jax.ShapeDtypeStruct((s, width), BF16),
        scratch_shapes=[
            pltpu.VMEM((2, 1, tile), F32),
            pltpu.VMEM((2, 2, VT_ROWS, tile), F32),
            pltpu.VMEM((2, tile, tile), F32),
            pltpu.VMEM((2, 2, tile, tile), BF16),
            pltpu.VMEM((2, 2, 1, tile), F32),
            pltpu.VMEM((2, 1, tile), F32),
        ],
        compiler_params=pltpu.CompilerParams(
            dimension_semantics=("arbitrary",), vmem_limit_bytes=V7X_VMEM_LIMIT_BYTES),
        name="diff_attn",
    )(lam_p, g_sub, qt, k, vt)


def _ffn_ln(h, wg_ref, wu_ref, wd_ref, act_sc, g2, b2):
    hb = h.astype(BF16)
    d_ff = wg_ref.shape[1]
    for c in range(d_ff // MXU_WIDTH):
        cols = slice(c * MXU_WIDTH, (c + 1) * MXU_WIDTH)
        gate = jnp.dot(hb, wg_ref[:, cols], preferred_element_type=F32)
        up = jnp.dot(hb, wu_ref[:, cols], preferred_element_type=F32)
        act_sc[:, cols] = (gate / (1.0 + jnp.exp(-gate)) * up).astype(BF16)
    y = jnp.dot(act_sc[...], wd_ref[...], preferred_element_type=F32)
    return _layer_norm(ALPHA * h + y, g2, b2)


def _post_attn_kernel(x_ref, o_ref, wo_ref, wg_ref, wu_ref, wd_ref, ln_ref, out_ref, act_sc):
    ln = ln_ref[...]
    m = jnp.dot(o_ref[...], wo_ref[...], preferred_element_type=F32)
    h = _layer_norm(ALPHA * x_ref[...] + m, ln[0:1], ln[1:2])
    out_ref[...] = _ffn_ln(h, wg_ref, wu_ref, wd_ref, act_sc, ln[2:3], ln[3:4])


def _post_attn_call(x2, o, w_o, w_gate, w_up, w_down, ln, *, tm):
    s, d = x2.shape
    d_ff = w_gate.shape[1]
    return pl.pallas_call(
        _post_attn_kernel,
        grid=(s // tm,),
        in_specs=[
            pl.BlockSpec((tm, d), lambda i: (i, 0)),
            pl.BlockSpec((tm, o.shape[1]), lambda i: (i, 0)),
            _resident(w_o.shape),
            _resident(w_gate.shape),
            _resident(w_up.shape),
            _resident(w_down.shape),
            _resident(ln.shape),
        ],
        out_specs=pl.BlockSpec((tm, d), lambda i: (i, 0)),
        out_shape=jax.ShapeDtypeStruct((s, d), F32),
        scratch_shapes=[pltpu.VMEM((tm, d_ff), BF16)],
        compiler_params=pltpu.CompilerParams(
            dimension_semantics=("arbitrary",), vmem_limit_bytes=V7X_VMEM_LIMIT_BYTES),
        name="attn_out_ffn",
    )(x2, o, w_o, w_gate, w_up, w_down, ln)


def _conv_layer_kernel(h_ref, win_ref, wc_ref, wout_ref, wg_ref, wu_ref, wd_ref, ln_ref, out_ref,
                       act_sc, tail_sc, *, tm, ch):
    i = pl.program_id(0)
    ln = ln_ref[...]
    h = h_ref[...]
    hb = h.astype(BF16)

    @pl.when(i == 0)
    def _():
        tail_sc[...] = jnp.zeros(tail_sc.shape, F32)

    b_gate = jnp.dot(hb, win_ref[:, 0:ch], preferred_element_type=F32)
    c_gate = jnp.dot(hb, win_ref[:, ch:2 * ch], preferred_element_type=F32)
    xt = jnp.dot(hb, win_ref[:, 2 * ch:3 * ch], preferred_element_type=F32)
    u = c_gate * xt

    row = lax.broadcasted_iota(jnp.int32, (tm, ch), 0)
    tail = tail_sc[...]
    wc = wc_ref[...]
    y = wc[CONV_WIDTH - 1:CONV_WIDTH] * u
    for d in range(1, CONV_WIDTH):
        shifted = pltpu.roll(u, d, 0)
        for t in range(d):
            src = CONV_WIDTH - 1 - d + t
            shifted = jnp.where(row == t, tail[src:src + 1], shifted)
        y = y + wc[CONV_WIDTH - 1 - d:CONV_WIDTH - d] * shifted
    tail_sc[...] = u[tm - (CONV_WIDTH - 1):]

    m = jnp.dot((b_gate * y).astype(BF16), wout_ref[...], preferred_element_type=F32)
    h1 = _layer_norm(ALPHA * h + m, ln[0:1], ln[1:2])
    out_ref[...] = _ffn_ln(h1, wg_ref, wu_ref, wd_ref, act_sc, ln[2:3], ln[3:4])


def _conv_layer_call(h, w_in, w_conv, w_out, w_gate, w_up, w_down, ln, *, tm):
    s, d = h.shape
    ch = w_out.shape[0]
    d_ff = w_gate.shape[1]
    return pl.pallas_call(
        functools.partial(_conv_layer_kernel, tm=tm, ch=ch),
        grid=(s // tm,),
        in_specs=[
            pl.BlockSpec((tm, d), lambda i: (i, 0)),
            _resident(w_in.shape),
            _resident(w_conv.shape),
            _resident(w_out.shape),
            _resident(w_gate.shape),
            _resident(w_up.shape),
            _resident(w_down.shape),
            _resident(ln.shape),
        ],
        out_specs=pl.BlockSpec((tm, d), lambda i: (i, 0)),
        out_shape=jax.ShapeDtypeStruct((s, d), F32),
        scratch_shapes=[pltpu.VMEM((tm, d_ff), BF16), pltpu.VMEM((CONV_WIDTH - 1, ch), F32)],
        compiler_params=pltpu.CompilerParams(
            dimension_semantics=("arbitrary",), vmem_limit_bytes=V7X_VMEM_LIMIT_BYTES),
        name="conv_ffn",
    )(h, w_in, w_conv, w_out, w_gate, w_up, w_down, ln)


def _rope_inv_lanes():
    inv = ROPE_THETA ** (-jnp.arange(0, ROPE_DIM, 2, dtype=F32) / ROPE_DIM)
    per_comp = jnp.concatenate([inv, inv, jnp.zeros((HEAD_DIM - ROPE_DIM,), F32)])
    return jnp.tile(per_comp, HEAD_W // HEAD_DIM)[None, :]


def kernel(x, attn_w_qkv, attn_w_o, attn_lambda_q1, attn_lambda_k1, attn_lambda_q2, attn_lambda_k2,
           attn_subln_g, conv_w_in, conv_w, conv_w_out, ffn_w_gate, ffn_w_up, ffn_w_down,
           ln1_g, ln1_b, ln2_g, ln2_b):
    b, s, d = x.shape
    assert b == 1 and attn_w_qkv.shape[0] == 1 and conv_w_in.shape[0] == 1
    tm = min(512, s)
    tile = min(1024, s)
    x2 = x.reshape(s, d)

    qt, k, vt = _qkv_call(x2, attn_w_qkv[0].astype(BF16), _rope_inv_lanes(), tm=tm)
    lam_p = jnp.concatenate([attn_lambda_q1, attn_lambda_k1, attn_lambda_q2, attn_lambda_k2], axis=0)
    o = _attn_call(lam_p, attn_subln_g, qt, k, vt, tile=tile, lam_init=LAMBDA_INIT_0)

    ln0 = jnp.stack([ln1_g[0], ln1_b[0], ln2_g[0], ln2_b[0]])
    h = _post_attn_call(x2, o, attn_w_o[0].astype(BF16), ffn_w_gate[0].astype(BF16),
                        ffn_w_up[0].astype(BF16), ffn_w_down[0].astype(BF16), ln0, tm=tm)

    ln1 = jnp.stack([ln1_g[1], ln1_b[1], ln2_g[1], ln2_b[1]])
    out = _conv_layer_call(h, conv_w_in[0].astype(BF16), conv_w[0], conv_w_out[0].astype(BF16),
                           ffn_w_gate[1].astype(BF16), ffn_w_up[1].astype(BF16),
                           ffn_w_down[1].astype(BF16), ln1, tm=tm)
    return out.reshape(b, s, d)
```

```python
import functools
import math

import jax
import jax.numpy as jnp
from jax import lax
from jax.experimental import pallas as pl
from jax.experimental.pallas import tpu as pltpu

F32 = jnp.float32
BF16 = jnp.bfloat16

CHUNK = 64
N_HEADS = 8
HEAD_DIM = 64
HEAD_W = 2 * HEAD_DIM
ROPE_DIM = HEAD_DIM // 4
ROPE_HALF = ROPE_DIM // 2
ROPE_THETA = 500000.0
CONV_WIDTH = 3
DEPTH = 2
ALPHA = (2 * DEPTH) ** 0.25
LN_EPS = 1e-5
NEG_INF = -1e30
LAMBDA_INIT_0 = 0.8 - 0.6 * math.exp(-0.3 * 0)

V7X_VMEM_LIMIT_BYTES = 56 * 1024 * 1024
MXU_WIDTH = 256
ROW_GROUPS = 2
BF16_SUBLANES = 16
VT_ROWS = HEAD_W + BF16_SUBLANES
Q_SCALE = HEAD_DIM ** -0.5 * math.log2(math.e)


def _resident(shape):
    return pl.BlockSpec(shape, lambda *_: (0,) * len(shape), pipeline_mode=pl.Buffered(1))


def _layer_norm(x, g, b):
    mu = jnp.mean(x, axis=-1, keepdims=True)
    xc = x - mu
    var = jnp.mean(xc * xc, axis=-1, keepdims=True)
    return xc * lax.rsqrt(var + LN_EPS) * g + b


def _qkv_kernel(x_ref, w_ref, inv_ref, qt_ref, k_ref, vt_ref, cos_sc, sin_sc, *, tm, width):
    i = pl.program_id(0)
    xb = x_ref[...].astype(BF16)
    inv = inv_ref[...]

    @pl.when(i == 0)
    def _():
        off = lax.broadcasted_iota(jnp.int32, (tm, HEAD_W), 0).astype(F32) * inv
        cos_sc[...] = jnp.cos(off)
        sin_sc[...] = jnp.sin(off)

    base = (i * tm).astype(F32) * inv
    cos_b, sin_b = jnp.cos(base), jnp.sin(base)
    cos_r, sin_r = cos_sc[...], sin_sc[...]
    cos = cos_b * cos_r - sin_b * sin_r
    sin = sin_b * cos_r + cos_b * sin_r
    lane = lax.broadcasted_iota(jnp.int32, (tm, HEAD_W), 1)
    w = lane % HEAD_DIM
    c_tab = jnp.where(w < ROPE_DIM, cos, 1.0)
    s_up = jnp.where(w < ROPE_HALF, -sin, 0.0)
    s_dn = jnp.where((w >= ROPE_HALF) & (w < ROPE_DIM), sin, 0.0)
    first_t = lax.broadcasted_iota(jnp.int32, (HEAD_W, tm), 0) < HEAD_DIM

    def rope(t):
        up = pltpu.roll(t, HEAD_W - ROPE_HALF, 1)
        dn = pltpu.roll(t, ROPE_HALF, 1)
        return t * c_tab + up * s_up + dn * s_dn

    def proj(col):
        return jnp.dot(xb, w_ref[:, col:col + MXU_WIDTH], preferred_element_type=F32)

    ones_row = (lax.broadcasted_iota(jnp.int32, (BF16_SUBLANES, tm), 0) == 0).astype(BF16)

    for c in range(width // MXU_WIDTH):
        q2 = proj(c * MXU_WIDTH)
        k2 = proj(width + c * MXU_WIDTH)
        v2 = proj(2 * width + c * MXU_WIDTH)
        for half in range(MXU_WIDTH // HEAD_W):
            head = c * (MXU_WIDTH // HEAD_W) + half
            lo = head * HEAD_W
            cols = slice(half * HEAD_W, (half + 1) * HEAD_W)
            qh_t = (rope(q2[:, cols]) * Q_SCALE).T
            qt_ref[0, lo:lo + HEAD_W, :] = jnp.where(first_t, qh_t, 0.0).astype(BF16)
            qt_ref[1, lo:lo + HEAD_W, :] = jnp.where(first_t, 0.0, qh_t).astype(BF16)
            k_ref[:, lo:lo + HEAD_W] = rope(k2[:, cols]).astype(BF16)
            vlo = head * VT_ROWS
            vt_ref[vlo:vlo + HEAD_W, :] = v2[:, cols].T.astype(BF16)
            vt_ref[vlo + HEAD_W:vlo + VT_ROWS, :] = ones_row


def _qkv_call(x2, w_qkv, inv_lane, *, tm):
    s, d = x2.shape
    width = w_qkv.shape[1] // 3
    return pl.pallas_call(
        functools.partial(_qkv_kernel, tm=tm, width=width),
        grid=(s // tm,),
        in_specs=[
            pl.BlockSpec((tm, d), lambda i: (i, 0)),
            _resident(w_qkv.shape),
            _resident(inv_lane.shape),
        ],
        out_specs=[
            pl.BlockSpec((2, width, tm), lambda i: (0, 0, i)),
            pl.BlockSpec((tm, width), lambda i: (i, 0)),
            pl.BlockSpec((N_HEADS * VT_ROWS, tm), lambda i: (0, i)),
        ],
        out_shape=[
            jax.ShapeDtypeStruct((2, width, s), BF16),
            jax.ShapeDtypeStruct((s, width), BF16),
            jax.ShapeDtypeStruct((N_HEADS * VT_ROWS, s), BF16),
        ],
        scratch_shapes=[pltpu.VMEM((tm, HEAD_W), F32), pltpu.VMEM((tm, HEAD_W), F32)],
        compiler_params=pltpu.CompilerParams(
            dimension_semantics=("arbitrary",), vmem_limit_bytes=V7X_VMEM_LIMIT_BYTES),
        name="qkv_rope",
    )(x2, w_qkv, inv_lane)


def _attn_kernel(lam_ref, g_ref, qt_ref, k_ref, vt_ref, o_ref, m_sc, acc_sc, s_sc, p_sc, a_sc, smax_sc,
                 *, tile, n_blocks, lam_init):
    n_pairs = n_blocks * (n_blocks + 1) // 2
    n_steps = n_pairs // 2 + 1

    m_sc[...] = jnp.full(m_sc.shape, NEG_INF, F32)
    acc_sc[...] = jnp.zeros(acc_sc.shape, F32)
    p_sc[...] = jnp.zeros(p_sc.shape, BF16)
    a_sc[...] = jnp.ones(a_sc.shape, F32)

    key = lax.broadcasted_iota(jnp.int32, (tile, tile), 0)
    qry = lax.broadcasted_iota(jnp.int32, (tile, tile), 1)
    diag_mask = key // CHUNK <= qry // CHUNK

    def clamped(x):
        return jnp.minimum(x, n_blocks - 1)

    def scores(i, t):
        k_t = k_ref[pl.ds(pl.multiple_of(t * tile, tile), tile), :]
        q_lo = pl.multiple_of(i * tile, tile)
        for c in range(2):
            s = jnp.dot(k_t, qt_ref[c, :, pl.ds(q_lo, tile)], preferred_element_type=F32)
            s_sc[c] = s
            smax_sc[c] = jnp.max(s, axis=0, keepdims=True)

    def softmax(slot, t, masked):
        for c in range(2):
            s = s_sc[c]
            if masked:
                s = jnp.where(diag_mask, s, NEG_INF)
                s_max = jnp.max(s, axis=0, keepdims=True)
            else:
                s_max = smax_sc[c]
            m_prev = jnp.where(t == 0, NEG_INF, m_sc[c])
            m_new = jnp.maximum(m_prev, s_max)
            p_sc[slot, c] = jnp.exp2(s - m_new).astype(BF16)
            a_sc[slot, c] = jnp.exp2(m_prev - m_new)
            m_sc[c] = m_new

    def accumulate(slot, i, t):
        v_t = vt_ref[:, pl.ds(pl.multiple_of(clamped(t) * tile, tile), tile)]
        par = i % 2
        for c in range(2):
            acc_sc[par, c] = a_sc[slot, c] * acc_sc[par, c] + jnp.dot(
                v_t, p_sc[slot, c], preferred_element_type=F32)

    def finalize(i):
        lam_p = lam_ref[...]
        lam = (jnp.exp(jnp.sum(lam_p[0:1] * lam_p[1:2], axis=1, keepdims=True))
               - jnp.exp(jnp.sum(lam_p[2:3] * lam_p[3:4], axis=1, keepdims=True)) + lam_init)
        par = i % 2

        def normalised(c):
            return acc_sc[par, c, :HEAD_W] / acc_sc[par, c, HEAD_W:HEAD_W + 1]

        o = (normalised(0) - lam * normalised(1)).T
        o = o * lax.rsqrt(jnp.mean(o * o, axis=-1, keepdims=True) + LN_EPS) * g_ref[...]
        o_ref[pl.ds(pl.multiple_of(i * tile, tile), tile), :] = (o * (1.0 - lam_init)).astype(o_ref.dtype)

    def following(i, t):
        last = t == i
        return jnp.where(last, i + 1, i), jnp.where(last, 0, t + 1)

    scores(0, 0)

    def step(_, carry):
        i3, t3, i2, t2, i1, t1, i0, t0, done = carry

        @pl.when(done >= 0)
        def _():
            finalize(done)

        def stages(diag_a, diag_b):
            accumulate(1, i3, t3)
            softmax(0, t2, diag_a)
            scores(clamped(i1), clamped(t1))
            accumulate(0, i2, t2)
            softmax(1, t1, diag_b)
            scores(clamped(i0), clamped(t0))

        variant = jnp.where(t2 == i2, 1, jnp.where(t1 == i1, 2, 0))
        lax.switch(variant, [lambda: stages(False, False), lambda: stages(True, False),
                             lambda: stages(False, True)])
        done = jnp.where(t3 == i3, i3, jnp.where(t2 == i2, i2, -1))
        done = jnp.where(done < n_blocks, done, -1)
        nxt = following(i0, t0)
        return (i1, t1, i0, t0) + nxt + following(*nxt) + (done,)

    zero = jnp.int32(0)
    first = (zero, zero)
    second = following(*first)
    carry = lax.fori_loop(0, n_steps, step, first + first + second + following(*second) + (jnp.int32(-1),))

    @pl.when(carry[-1] >= 0)
    def _():
        finalize(carry[-1])


def _attn_call(lam_p, g_sub, qt, k, vt, *, tile, lam_init):
    s, width = k.shape
    return pl.pallas_call(
        functools.partial(_attn_kernel, tile=tile, n_blocks=s // tile, lam_init=lam_init),
        grid=(N_HEADS,),
        in_specs=[
            pl.BlockSpec(lam_p.shape, lambda h: (0, 0)),
            pl.BlockSpec(g_sub.shape, lambda h: (0, 0)),
            pl.BlockSpec((2, HEAD_W, s), lambda h: (0, h, 0), pipeline_mode=pl.Buffered(1)),
            pl.BlockSpec((s, HEAD_W), lambda h: (0, h)),
            pl.BlockSpec((VT_ROWS, s), lambda h: (h, 0), pipeline_mode=pl.Buffered(1)),
        ],
        out_specs=pl.BlockSpec((s, HEAD_W), lambda h: (0, h)),
        out_shape=jax.ShapeDtypeStruct((s, width), BF16),
        scratch_shapes=[
            pltpu.VMEM((2, 1, tile), F32),
            pltpu.VMEM((2, 2, VT_ROWS, tile), F32),
            pltpu.VMEM((2, tile, tile), F32),
            pltpu.VMEM((2, 2, tile, tile), BF16),
            pltpu.VMEM((2, 2, 1, tile), F32),
            pltpu.VMEM((2, 1, tile), F32),
        ],
        compiler_params=pltpu.CompilerParams(
            dimension_semantics=("arbitrary",), vmem_limit_bytes=V7X_VMEM_LIMIT_BYTES),
        name="diff_attn",
    )(lam_p, g_sub, qt, k, vt)


def _swiglu(h, wg_ref, wu_ref, wd_ref, act_sc, rows):
    hb = h.astype(BF16)
    d_ff = wg_ref.shape[1]
    for c in range(d_ff // MXU_WIDTH):
        cols = slice(c * MXU_WIDTH, (c + 1) * MXU_WIDTH)
        gate = jnp.dot(hb, wg_ref[:, cols], preferred_element_type=F32)
        up = jnp.dot(hb, wu_ref[:, cols], preferred_element_type=F32)
        act_sc[rows, cols] = (gate / (1.0 + jnp.exp(-gate)) * up).astype(BF16)
    return jnp.dot(act_sc[rows, :], wd_ref[...], preferred_element_type=F32)


def _row_groups(tm):
    rows = tm // ROW_GROUPS
    return [slice(g * rows, (g + 1) * rows) for g in range(ROW_GROUPS)]


def _mixer_out_ffn(resid, mixed, w_ref, wg_ref, wu_ref, wd_ref, ln, out_ref, act_sc, groups):
    m = [jnp.dot(mixed(r), w_ref[...], preferred_element_type=F32) for r in groups]
    h = [_layer_norm(ALPHA * resid(r) + m_g, ln[0:1], ln[1:2]) for r, m_g in zip(groups, m)]
    y = [_swiglu(h_g, wg_ref, wu_ref, wd_ref, act_sc, r) for r, h_g in zip(groups, h)]
    for r, h_g, y_g in zip(groups, h, y):
        out_ref[r, :] = _layer_norm(ALPHA * h_g + y_g, ln[2:3], ln[3:4])


def _post_attn_kernel(x_ref, o_ref, wo_ref, wg_ref, wu_ref, wd_ref, ln_ref, out_ref, act_sc):
    ln = ln_ref[...]
    _mixer_out_ffn(lambda r: x_ref[r, :], lambda r: o_ref[r, :], wo_ref, wg_ref, wu_ref, wd_ref, ln,
                   out_ref, act_sc, _row_groups(x_ref.shape[0]))


def _post_attn_call(x2, o, w_o, w_gate, w_up, w_down, ln, *, tm):
    s, d = x2.shape
    d_ff = w_gate.shape[1]
    return pl.pallas_call(
        _post_attn_kernel,
        grid=(s // tm,),
        in_specs=[
            pl.BlockSpec((tm, d), lambda i: (i, 0)),
            pl.BlockSpec((tm, o.shape[1]), lambda i: (i, 0)),
            _resident(w_o.shape),
            _resident(w_gate.shape),
            _resident(w_up.shape),
            _resident(w_down.shape),
            _resident(ln.shape),
        ],
        out_specs=pl.BlockSpec((tm, d), lambda i: (i, 0)),
        out_shape=jax.ShapeDtypeStruct((s, d), F32),
        scratch_shapes=[pltpu.VMEM((tm, d_ff), BF16)],
        compiler_params=pltpu.CompilerParams(
            dimension_semantics=("arbitrary",), vmem_limit_bytes=V7X_VMEM_LIMIT_BYTES),
        name="attn_out_ffn",
    )(x2, o, w_o, w_gate, w_up, w_down, ln)


def _conv_layer_kernel(h_ref, win_ref, wc_ref, wout_ref, wg_ref, wu_ref, wd_ref, ln_ref, out_ref,
                       act_sc, tail_sc, *, tm, ch):
    i = pl.program_id(0)
    ln = ln_ref[...]
    h = h_ref[...]
    hb = h.astype(BF16)

    @pl.when(i == 0)
    def _():
        tail_sc[...] = jnp.zeros(tail_sc.shape, F32)

    b_gate = jnp.dot(hb, win_ref[:, 0:ch], preferred_element_type=F32)
    c_gate = jnp.dot(hb, win_ref[:, ch:2 * ch], preferred_element_type=F32)
    xt = jnp.dot(hb, win_ref[:, 2 * ch:3 * ch], preferred_element_type=F32)
    u = c_gate * xt

    row = lax.broadcasted_iota(jnp.int32, (tm, ch), 0)
    tail = tail_sc[...]
    wc = wc_ref[...]
    y = wc[CONV_WIDTH - 1:CONV_WIDTH] * u
    for d in range(1, CONV_WIDTH):
        shifted = pltpu.roll(u, d, 0)
        for t in range(d):
            src = CONV_WIDTH - 1 - d + t
            shifted = jnp.where(row == t, tail[src:src + 1], shifted)
        y = y + wc[CONV_WIDTH - 1 - d:CONV_WIDTH - d] * shifted
    tail_sc[...] = u[tm - (CONV_WIDTH - 1):]

    mixed = (b_gate * y).astype(BF16)
    _mixer_out_ffn(lambda r: h[r, :], lambda r: mixed[r, :], wout_ref, wg_ref, wu_ref, wd_ref, ln,
                   out_ref, act_sc, _row_groups(tm))


def _conv_layer_call(h, w_in, w_conv, w_out, w_gate, w_up, w_down, ln, *, tm):
    s, d = h.shape
    ch = w_out.shape[0]
    d_ff = w_gate.shape[1]
    return pl.pallas_call(
        functools.partial(_conv_layer_kernel, tm=tm, ch=ch),
        grid=(s // tm,),
        in_specs=[
            pl.BlockSpec((tm, d), lambda i: (i, 0)),
            _resident(w_in.shape),
            _resident(w_conv.shape),
            _resident(w_out.shape),
            _resident(w_gate.shape),
            _resident(w_up.shape),
            _resident(w_down.shape),
            _resident(ln.shape),
        ],
        out_specs=pl.BlockSpec((tm, d), lambda i: (i, 0)),
        out_shape=jax.ShapeDtypeStruct((s, d), F32),
        scratch_shapes=[pltpu.VMEM((tm, d_ff), BF16), pltpu.VMEM((CONV_WIDTH - 1, ch), F32)],
        compiler_params=pltpu.CompilerParams(
            dimension_semantics=("arbitrary",), vmem_limit_bytes=V7X_VMEM_LIMIT_BYTES),
        name="conv_ffn",
    )(h, w_in, w_conv, w_out, w_gate, w_up, w_down, ln)


def _rope_inv_lanes():
    inv = ROPE_THETA ** (-jnp.arange(0, ROPE_DIM, 2, dtype=F32) / ROPE_DIM)
    per_comp = jnp.concatenate([inv, inv, jnp.zeros((HEAD_DIM - ROPE_DIM,), F32)])
    return jnp.tile(per_comp, HEAD_W // HEAD_DIM)[None, :]


def kernel(x, attn_w_qkv, attn_w_o, attn_lambda_q1, attn_lambda_k1, attn_lambda_q2, attn_lambda_k2,
           attn_subln_g, conv_w_in, conv_w, conv_w_out, ffn_w_gate, ffn_w_up, ffn_w_down,
           ln1_g, ln1_b, ln2_g, ln2_b):
    b, s, d = x.shape
    assert b == 1 and attn_w_qkv.shape[0] == 1 and conv_w_in.shape[0] == 1
    tm = min(512, s)
    tile = min(1024, s)
    x2 = x.reshape(s, d)

    qt, k, vt = _qkv_call(x2, attn_w_qkv[0].astype(BF16), _rope_inv_lanes(), tm=tm)
    lam_p = jnp.concatenate([attn_lambda_q1, attn_lambda_k1, attn_lambda_q2, attn_lambda_k2], axis=0)
    o = _attn_call(lam_p, attn_subln_g, qt, k, vt, tile=tile, lam_init=LAMBDA_INIT_0)

    ln0 = jnp.stack([ln1_g[0], ln1_b[0], ln2_g[0], ln2_b[0]])
    h = _post_attn_call(x2, o, attn_w_o[0].astype(BF16), ffn_w_gate[0].astype(BF16),
                        ffn_w_up[0].astype(BF16), ffn_w_down[0].astype(BF16), ln0, tm=tm)

    ln1 = jnp.stack([ln1_g[1], ln1_b[1], ln2_g[1], ln2_b[1]])
    out = _conv_layer_call(h, conv_w_in[0].astype(BF16), conv_w[0], conv_w_out[0].astype(BF16),
                           ffn_w_gate[1].astype(BF16), ffn_w_up[1].astype(BF16),
                           ffn_w_down[1].astype(BF16), ln1, tm=tm)
    return out.reshape(b, s, d)
```

```python
import functools
import math

import jax
import jax.numpy as jnp
from jax import lax
from jax.experimental import pallas as pl
from jax.experimental.pallas import tpu as pltpu

F32 = jnp.float32
BF16 = jnp.bfloat16

CHUNK = 64
N_HEADS = 8
HEAD_DIM = 64
HEAD_W = 2 * HEAD_DIM
ROPE_DIM = HEAD_DIM // 4
ROPE_HALF = ROPE_DIM // 2
ROPE_THETA = 500000.0
CONV_WIDTH = 3
DEPTH = 2
ALPHA = (2 * DEPTH) ** 0.25
LN_EPS = 1e-5
NEG_INF = -1e30
LAMBDA_INIT_0 = 0.8 - 0.6 * math.exp(-0.3 * 0)

V7X_VMEM_LIMIT_BYTES = 56 * 1024 * 1024
MXU_WIDTH = 256
ROW_GROUPS = 2
BF16_SUBLANES = 16
VT_ROWS = HEAD_W + BF16_SUBLANES
Q_SCALE = HEAD_DIM ** -0.5 * math.log2(math.e)


def _resident(shape):
    return pl.BlockSpec(shape, lambda *_: (0,) * len(shape), pipeline_mode=pl.Buffered(1))


def _resident_layer(stacked, layer):
    return pl.BlockSpec((None,) + stacked.shape[1:], lambda *_: (layer, 0, 0), pipeline_mode=pl.Buffered(1))


def _cast_spec(stacked, n_steps):
    rows = stacked.shape[1]
    n_blk = max(d for d in range(1, n_steps + 1) if rows % d == 0 and (rows // d) % BF16_SUBLANES == 0)
    block = (stacked.shape[0], rows // n_blk, stacked.shape[2])
    return pl.BlockSpec(block, lambda i: (0, i * n_blk // n_steps, 0))


def _layer_norm(x, g, b):
    mu = jnp.mean(x, axis=-1, keepdims=True)
    xc = x - mu
    var = jnp.mean(xc * xc, axis=-1, keepdims=True)
    return xc * lax.rsqrt(var + LN_EPS) * g + b


def _qkv_kernel(x_ref, w_ref, inv_ref, *rest, tm, width, n_cast):
    cast_src, (qt_ref, k_ref, vt_ref), cast_dst, (cos_sc, sin_sc, wq_sc) = (
        rest[:n_cast], rest[n_cast:n_cast + 3], rest[n_cast + 3:2 * n_cast + 3], rest[2 * n_cast + 3:])
    i = pl.program_id(0)
    xb = x_ref[...].astype(BF16)
    inv = inv_ref[...]

    for src, dst in zip(cast_src, cast_dst):
        dst[...] = src[...].astype(BF16)

    @pl.when(i == 0)
    def _():
        wq_sc[...] = w_ref[...].astype(BF16)
        off = lax.broadcasted_iota(jnp.int32, (tm, HEAD_W), 0).astype(F32) * inv
        cos_sc[...] = jnp.cos(off)
        sin_sc[...] = jnp.sin(off)

    base = (i * tm).astype(F32) * inv
    cos_b, sin_b = jnp.cos(base), jnp.sin(base)
    cos_r, sin_r = cos_sc[...], sin_sc[...]
    cos = cos_b * cos_r - sin_b * sin_r
    sin = sin_b * cos_r + cos_b * sin_r
    lane = lax.broadcasted_iota(jnp.int32, (tm, HEAD_W), 1)
    w = lane % HEAD_DIM
    c_tab = jnp.where(w < ROPE_DIM, cos, 1.0)
    s_up = jnp.where(w < ROPE_HALF, -sin, 0.0)
    s_dn = jnp.where((w >= ROPE_HALF) & (w < ROPE_DIM), sin, 0.0)
    first_t = lax.broadcasted_iota(jnp.int32, (HEAD_W, tm), 0) < HEAD_DIM

    def rope(t):
        up = pltpu.roll(t, HEAD_W - ROPE_HALF, 1)
        dn = pltpu.roll(t, ROPE_HALF, 1)
        return t * c_tab + up * s_up + dn * s_dn

    def proj(col):
        return jnp.dot(xb, wq_sc[:, col:col + MXU_WIDTH], preferred_element_type=F32)

    ones_row = (lax.broadcasted_iota(jnp.int32, (BF16_SUBLANES, tm), 0) == 0).astype(BF16)

    for c in range(width // MXU_WIDTH):
        q2 = proj(c * MXU_WIDTH)
        k2 = proj(width + c * MXU_WIDTH)
        v2 = proj(2 * width + c * MXU_WIDTH)
        for half in range(MXU_WIDTH // HEAD_W):
            head = c * (MXU_WIDTH // HEAD_W) + half
            lo = head * HEAD_W
            cols = slice(half * HEAD_W, (half + 1) * HEAD_W)
            qh_t = (rope(q2[:, cols]) * Q_SCALE).T
            qt_ref[0, lo:lo + HEAD_W, :] = jnp.where(first_t, qh_t, 0.0).astype(BF16)
            qt_ref[1, lo:lo + HEAD_W, :] = jnp.where(first_t, 0.0, qh_t).astype(BF16)
            k_ref[:, lo:lo + HEAD_W] = rope(k2[:, cols]).astype(BF16)
            vlo = head * VT_ROWS
            vt_ref[vlo:vlo + HEAD_W, :] = v2[:, cols].T.astype(BF16)
            vt_ref[vlo + HEAD_W:vlo + VT_ROWS, :] = ones_row


def _qkv_call(x2, w_qkv, inv_lane, later_weights, *, tm):
    s, d = x2.shape
    width = w_qkv.shape[2] // 3
    n_steps = s // tm
    cast_specs = [_cast_spec(w, n_steps) for w in later_weights]
    outs = pl.pallas_call(
        functools.partial(_qkv_kernel, tm=tm, width=width, n_cast=len(later_weights)),
        grid=(n_steps,),
        in_specs=[
            pl.BlockSpec((tm, d), lambda i: (i, 0)),
            _resident_layer(w_qkv, 0),
            _resident(inv_lane.shape),
            *cast_specs,
        ],
        out_specs=[
            pl.BlockSpec((2, width, tm), lambda i: (0, 0, i)),
            pl.BlockSpec((tm, width), lambda i: (i, 0)),
            pl.BlockSpec((N_HEADS * VT_ROWS, tm), lambda i: (0, i)),
            *cast_specs,
        ],
        out_shape=[
            jax.ShapeDtypeStruct((2, width, s), BF16),
            jax.ShapeDtypeStruct((s, width), BF16),
            jax.ShapeDtypeStruct((N_HEADS * VT_ROWS, s), BF16),
            *[jax.ShapeDtypeStruct(w.shape, BF16) for w in later_weights],
        ],
        scratch_shapes=[pltpu.VMEM((tm, HEAD_W), F32), pltpu.VMEM((tm, HEAD_W), F32),
                        pltpu.VMEM(w_qkv.shape[1:], BF16)],
        compiler_params=pltpu.CompilerParams(
            dimension_semantics=("arbitrary",), vmem_limit_bytes=V7X_VMEM_LIMIT_BYTES),
        name="qkv_rope",
    )(x2, w_qkv, inv_lane, *later_weights)
    return outs[:3], outs[3:]


def _attn_kernel(lam_ref, g_ref, qt_ref, k_ref, vt_ref, o_ref, m_sc, acc_sc, s_sc, p_sc, a_sc, smax_sc,
                 *, tile, n_blocks, lam_init):
    n_pairs = n_blocks * (n_blocks + 1) // 2
    n_steps = n_pairs // 2 + 1

    m_sc[...] = jnp.full(m_sc.shape, NEG_INF, F32)
    acc_sc[...] = jnp.zeros(acc_sc.shape, F32)
    p_sc[...] = jnp.zeros(p_sc.shape, BF16)
    a_sc[...] = jnp.ones(a_sc.shape, F32)

    key = lax.broadcasted_iota(jnp.int32, (tile, tile), 0)
    qry = lax.broadcasted_iota(jnp.int32, (tile, tile), 1)
    diag_mask = key // CHUNK <= qry // CHUNK

    def clamped(x):
        return jnp.minimum(x, n_blocks - 1)

    def scores(i, t):
        k_t = k_ref[pl.ds(pl.multiple_of(t * tile, tile), tile), :]
        q_lo = pl.multiple_of(i * tile, tile)
        for c in range(2):
            s = jnp.dot(k_t, qt_ref[c, :, pl.ds(q_lo, tile)], preferred_element_type=F32)
            s_sc[c] = s
            smax_sc[c] = jnp.max(s, axis=0, keepdims=True)

    def softmax(slot, t, masked):
        for c in range(2):
            s = s_sc[c]
            if masked:
                s = jnp.where(diag_mask, s, NEG_INF)
                s_max = jnp.max(s, axis=0, keepdims=True)
            else:
                s_max = smax_sc[c]
            m_prev = jnp.where(t == 0, NEG_INF, m_sc[c])
            m_new = jnp.maximum(m_prev, s_max)
            p_sc[slot, c] = jnp.exp2(s - m_new).astype(BF16)
            a_sc[slot, c] = jnp.exp2(m_prev - m_new)
            m_sc[c] = m_new

    def accumulate(slot, i, t):
        v_t = vt_ref[:, pl.ds(pl.multiple_of(clamped(t) * tile, tile), tile)]
        par = i % 2
        for c in range(2):
            acc_sc[par, c] = a_sc[slot, c] * acc_sc[par, c] + jnp.dot(
                v_t, p_sc[slot, c], preferred_element_type=F32)

    def finalize(i):
        lam_p = lam_ref[...]
        lam = (jnp.exp(jnp.sum(lam_p[0:1] * lam_p[1:2], axis=1, keepdims=True))
               - jnp.exp(jnp.sum(lam_p[2:3] * lam_p[3:4], axis=1, keepdims=True)) + lam_init)
        par = i % 2

        def normalised(c):
            return acc_sc[par, c, :HEAD_W] / acc_sc[par, c, HEAD_W:HEAD_W + 1]

        o = (normalised(0) - lam * normalised(1)).T
        o = o * lax.rsqrt(jnp.mean(o * o, axis=-1, keepdims=True) + LN_EPS) * g_ref[...]
        o_ref[pl.ds(pl.multiple_of(i * tile, tile), tile), :] = (o * (1.0 - lam_init)).astype(o_ref.dtype)

    def following(i, t):
        last = t == i
        return jnp.where(last, i + 1, i), jnp.where(last, 0, t + 1)

    scores(0, 0)

    def step(_, carry):
        i3, t3, i2, t2, i1, t1, i0, t0, done = carry

        @pl.when(done >= 0)
        def _():
            finalize(done)

        def stages(diag_a, diag_b):
            accumulate(1, i3, t3)
            softmax(0, t2, diag_a)
            scores(clamped(i1), clamped(t1))
            accumulate(0, i2, t2)
            softmax(1, t1, diag_b)
            scores(clamped(i0), clamped(t0))

        variant = jnp.where(t2 == i2, 1, jnp.where(t1 == i1, 2, 0))
        lax.switch(variant, [lambda: stages(False, False), lambda: stages(True, False),
                             lambda: stages(False, True)])
        done = jnp.where(t3 == i3, i3, jnp.where(t2 == i2, i2, -1))
        done = jnp.where(done < n_blocks, done, -1)
        nxt = following(i0, t0)
        return (i1, t1, i0, t0) + nxt + following(*nxt) + (done,)

    zero = jnp.int32(0)
    first = (zero, zero)
    second = following(*first)
    carry = lax.fori_loop(0, n_steps, step, first + first + second + following(*second) + (jnp.int32(-1),))

    @pl.when(carry[-1] >= 0)
    def _():
        finalize(carry[-1])


def _attn_call(lam_p, g_sub, qt, k, vt, *, tile, lam_init):
    s, width = k.shape
    return pl.pallas_call(
        functools.partial(_attn_kernel, tile=tile, n_blocks=s // tile, lam_init=lam_init),
        grid=(N_HEADS,),
        in_specs=[
            pl.BlockSpec(lam_p.shape, lambda h: (0, 0)),
            pl.BlockSpec(g_sub.shape, lambda h: (0, 0)),
            pl.BlockSpec((2, HEAD_W, s), lambda h: (0, h, 0), pipeline_mode=pl.Buffered(1)),
            pl.BlockSpec((s, HEAD_W), lambda h: (0, h)),
            pl.BlockSpec((VT_ROWS, s), lambda h: (h, 0), pipeline_mode=pl.Buffered(1)),
        ],
        out_specs=pl.BlockSpec((s, HEAD_W), lambda h: (0, h)),
        out_shape=jax.ShapeDtypeStruct((s, width), BF16),
        scratch_shapes=[
            pltpu.VMEM((2, 1, tile), F32),
            pltpu.VMEM((2, 2, VT_ROWS, tile), F32),
            pltpu.VMEM((2, tile, tile), F32),
            pltpu.VMEM((2, 2, tile, tile), BF16),
            pltpu.VMEM((2, 2, 1, tile), F32),
            pltpu.VMEM((2, 1, tile), F32),
        ],
        compiler_params=pltpu.CompilerParams(
            dimension_semantics=("arbitrary",), vmem_limit_bytes=V7X_VMEM_LIMIT_BYTES),
        name="diff_attn",
    )(lam_p, g_sub, qt, k, vt)


def _swiglu(h, wg_ref, wu_ref, wd_ref, act_sc, rows):
    hb = h.astype(BF16)
    d_ff = wg_ref.shape[1]
    for c in range(d_ff // MXU_WIDTH):
        cols = slice(c * MXU_WIDTH, (c + 1) * MXU_WIDTH)
        gate = jnp.dot(hb, wg_ref[:, cols], preferred_element_type=F32)
        up = jnp.dot(hb, wu_ref[:, cols], preferred_element_type=F32)
        act_sc[rows, cols] = (gate / (1.0 + jnp.exp(-gate)) * up).astype(BF16)
    return jnp.dot(act_sc[rows, :], wd_ref[...], preferred_element_type=F32)


def _row_groups(tm):
    rows = tm // ROW_GROUPS
    return [slice(g * rows, (g + 1) * rows) for g in range(ROW_GROUPS)]


def _mixer_out_ffn(resid, mixed, w_ref, wg_ref, wu_ref, wd_ref, ln, out_ref, act_sc, groups):
    m = [jnp.dot(mixed(r), w_ref[...], preferred_element_type=F32) for r in groups]
    h = [_layer_norm(ALPHA * resid(r) + m_g, ln[0:1], ln[1:2]) for r, m_g in zip(groups, m)]
    y = [_swiglu(h_g, wg_ref, wu_ref, wd_ref, act_sc, r) for r, h_g in zip(groups, h)]
    for r, h_g, y_g in zip(groups, h, y):
        out_ref[r, :] = _layer_norm(ALPHA * h_g + y_g, ln[2:3], ln[3:4])


def _post_attn_kernel(x_ref, o_ref, wo_ref, wg_ref, wu_ref, wd_ref, ln_ref, out_ref, act_sc):
    ln = ln_ref[...]
    _mixer_out_ffn(lambda r: x_ref[r, :], lambda r: o_ref[r, :], wo_ref, wg_ref, wu_ref, wd_ref, ln,
                   out_ref, act_sc, _row_groups(x_ref.shape[0]))


def _post_attn_call(x2, o, w_o, w_gate, w_up, w_down, ln, *, tm, layer):
    s, d = x2.shape
    d_ff = w_gate.shape[2]
    return pl.pallas_call(
        _post_attn_kernel,
        grid=(s // tm,),
        in_specs=[
            pl.BlockSpec((tm, d), lambda i: (i, 0)),
            pl.BlockSpec((tm, o.shape[1]), lambda i: (i, 0)),
            _resident_layer(w_o, 0),
            _resident_layer(w_gate, layer),
            _resident_layer(w_up, layer),
            _resident_layer(w_down, layer),
            _resident(ln.shape),
        ],
        out_specs=pl.BlockSpec((tm, d), lambda i: (i, 0)),
        out_shape=jax.ShapeDtypeStruct((s, d), F32),
        scratch_shapes=[pltpu.VMEM((tm, d_ff), BF16)],
        compiler_params=pltpu.CompilerParams(
            dimension_semantics=("arbitrary",), vmem_limit_bytes=V7X_VMEM_LIMIT_BYTES),
        name="attn_out_ffn",
    )(x2, o, w_o, w_gate, w_up, w_down, ln)


def _conv_layer_kernel(h_ref, win_ref, wc_ref, wout_ref, wg_ref, wu_ref, wd_ref, ln_ref, out_ref,
                       act_sc, tail_sc, *, tm, ch):
    i = pl.program_id(0)
    ln = ln_ref[...]
    h = h_ref[...]
    hb = h.astype(BF16)

    @pl.when(i == 0)
    def _():
        tail_sc[...] = jnp.zeros(tail_sc.shape, F32)

    b_gate = jnp.dot(hb, win_ref[:, 0:ch], preferred_element_type=F32)
    c_gate = jnp.dot(hb, win_ref[:, ch:2 * ch], preferred_element_type=F32)
    xt = jnp.dot(hb, win_ref[:, 2 * ch:3 * ch], preferred_element_type=F32)
    u = c_gate * xt

    row = lax.broadcasted_iota(jnp.int32, (tm, ch), 0)
    tail = tail_sc[...]
    wc = wc_ref[...]
    y = wc[CONV_WIDTH - 1:CONV_WIDTH] * u
    for d in range(1, CONV_WIDTH):
        shifted = pltpu.roll(u, d, 0)
        for t in range(d):
            src = CONV_WIDTH - 1 - d + t
            shifted = jnp.where(row == t, tail[src:src + 1], shifted)
        y = y + wc[CONV_WIDTH - 1 - d:CONV_WIDTH - d] * shifted
    tail_sc[...] = u[tm - (CONV_WIDTH - 1):]

    mixed = (b_gate * y).astype(BF16)
    _mixer_out_ffn(lambda r: h[r, :], lambda r: mixed[r, :], wout_ref, wg_ref, wu_ref, wd_ref, ln,
                   out_ref, act_sc, _row_groups(tm))


def _conv_layer_call(h, w_in, w_conv, w_out, w_gate, w_up, w_down, ln, *, tm, layer):
    s, d = h.shape
    ch = w_out.shape[1]
    d_ff = w_gate.shape[2]
    return pl.pallas_call(
        functools.partial(_conv_layer_kernel, tm=tm, ch=ch),
        grid=(s // tm,),
        in_specs=[
            pl.BlockSpec((tm, d), lambda i: (i, 0)),
            _resident_layer(w_in, 0),
            _resident(w_conv.shape),
            _resident_layer(w_out, 0),
            _resident_layer(w_gate, layer),
            _resident_layer(w_up, layer),
            _resident_layer(w_down, layer),
            _resident(ln.shape),
        ],
        out_specs=pl.BlockSpec((tm, d), lambda i: (i, 0)),
        out_shape=jax.ShapeDtypeStruct((s, d), F32),
        scratch_shapes=[pltpu.VMEM((tm, d_ff), BF16), pltpu.VMEM((CONV_WIDTH - 1, ch), F32)],
        compiler_params=pltpu.CompilerParams(
            dimension_semantics=("arbitrary",), vmem_limit_bytes=V7X_VMEM_LIMIT_BYTES),
        name="conv_ffn",
    )(h, w_in, w_conv, w_out, w_gate, w_up, w_down, ln)


def _rope_inv_lanes():
    inv = ROPE_THETA ** (-jnp.arange(0, ROPE_DIM, 2, dtype=F32) / ROPE_DIM)
    per_comp = jnp.concatenate([inv, inv, jnp.zeros((HEAD_DIM - ROPE_DIM,), F32)])
    return jnp.tile(per_comp, HEAD_W // HEAD_DIM)[None, :]


def kernel(x, attn_w_qkv, attn_w_o, attn_lambda_q1, attn_lambda_k1, attn_lambda_q2, attn_lambda_k2,
           attn_subln_g, conv_w_in, conv_w, conv_w_out, ffn_w_gate, ffn_w_up, ffn_w_down,
           ln1_g, ln1_b, ln2_g, ln2_b):
    b, s, d = x.shape
    assert b == 1 and attn_w_qkv.shape[0] == 1 and conv_w_in.shape[0] == 1
    tm = min(512, s)
    tile = min(1024, s)
    x2 = x.reshape(s, d)

    later = (attn_w_o, conv_w_in, conv_w_out, ffn_w_gate, ffn_w_up, ffn_w_down)
    (qt, k, vt), (w_o, w_in, w_out, w_gate, w_up, w_down) = _qkv_call(
        x2, attn_w_qkv, _rope_inv_lanes(), later, tm=tm)
    lam_p = jnp.concatenate([attn_lambda_q1, attn_lambda_k1, attn_lambda_q2, attn_lambda_k2], axis=0)
    o = _attn_call(lam_p, attn_subln_g, qt, k, vt, tile=tile, lam_init=LAMBDA_INIT_0)

    ln0 = jnp.stack([ln1_g[0], ln1_b[0], ln2_g[0], ln2_b[0]])
    h = _post_attn_call(x2, o, w_o, w_gate, w_up, w_down, ln0, tm=tm, layer=0)

    ln1 = jnp.stack([ln1_g[1], ln1_b[1], ln2_g[1], ln2_b[1]])
    out = _conv_layer_call(h, w_in, conv_w[0], w_out, w_gate, w_up, w_down, ln1, tm=tm, layer=1)
    return out.reshape(b, s, d)
```

```python
import functools
import math

import jax
import jax.numpy as jnp
from jax import lax
from jax.experimental import pallas as pl
from jax.experimental.pallas import tpu as pltpu

F32 = jnp.float32
BF16 = jnp.bfloat16

CHUNK = 64
N_HEADS = 8
HEAD_DIM = 64
HEAD_W = 2 * HEAD_DIM
ROPE_DIM = HEAD_DIM // 4
ROPE_HALF = ROPE_DIM // 2
ROPE_THETA = 500000.0
CONV_WIDTH = 3
DEPTH = 2
ALPHA = (2 * DEPTH) ** 0.25
LN_EPS = 1e-5
NEG_INF = -1e30
LAMBDA_INIT_0 = 0.8 - 0.6 * math.exp(-0.3 * 0)

V7X_VMEM_LIMIT_BYTES = 56 * 1024 * 1024
MXU_WIDTH = 256
Q_SLOTS = 3
ROW_GROUPS = 2
BF16_SUBLANES = 16
VT_ROWS = HEAD_W + BF16_SUBLANES
Q_SCALE = HEAD_DIM ** -0.5 * math.log2(math.e)


def _resident(shape):
    return pl.BlockSpec(shape, lambda *_: (0,) * len(shape), pipeline_mode=pl.Buffered(1))


def _resident_layer(stacked, layer):
    return pl.BlockSpec((None,) + stacked.shape[1:], lambda *_: (layer, 0, 0), pipeline_mode=pl.Buffered(1))


def _cast_spec(stacked, n_steps):
    rows = stacked.shape[1]
    n_blk = max(d for d in range(1, n_steps + 1) if rows % d == 0 and (rows // d) % BF16_SUBLANES == 0)
    block = (stacked.shape[0], rows // n_blk, stacked.shape[2])
    return pl.BlockSpec(block, lambda i: (0, i * n_blk // n_steps, 0))


def _layer_norm(x, g, b):
    mu = jnp.mean(x, axis=-1, keepdims=True)
    xc = x - mu
    var = jnp.mean(xc * xc, axis=-1, keepdims=True)
    return xc * lax.rsqrt(var + LN_EPS) * g + b


def _qkv_kernel(x_ref, w_ref, inv_ref, *rest, tm, width, n_cast):
    cast_src, (qt_ref, k_ref, vt_ref), cast_dst, (cos_sc, sin_sc, wq_sc) = (
        rest[:n_cast], rest[n_cast:n_cast + 3], rest[n_cast + 3:2 * n_cast + 3], rest[2 * n_cast + 3:])
    i = pl.program_id(0)
    xb = x_ref[...].astype(BF16)
    inv = inv_ref[...]

    for src, dst in zip(cast_src, cast_dst):
        dst[...] = src[...].astype(BF16)

    @pl.when(i == 0)
    def _():
        wq_sc[...] = w_ref[...].astype(BF16)
        off = lax.broadcasted_iota(jnp.int32, (tm, HEAD_W), 0).astype(F32) * inv
        cos_sc[...] = jnp.cos(off)
        sin_sc[...] = jnp.sin(off)

    base = (i * tm).astype(F32) * inv
    cos_b, sin_b = jnp.cos(base), jnp.sin(base)
    cos_r, sin_r = cos_sc[...], sin_sc[...]
    cos = cos_b * cos_r - sin_b * sin_r
    sin = sin_b * cos_r + cos_b * sin_r
    lane = lax.broadcasted_iota(jnp.int32, (tm, HEAD_W), 1)
    w = lane % HEAD_DIM
    c_tab = jnp.where(w < ROPE_DIM, cos, 1.0)
    s_up = jnp.where(w < ROPE_HALF, -sin, 0.0)
    s_dn = jnp.where((w >= ROPE_HALF) & (w < ROPE_DIM), sin, 0.0)
    first_t = lax.broadcasted_iota(jnp.int32, (HEAD_W, tm), 0) < HEAD_DIM

    def rope(t):
        up = pltpu.roll(t, HEAD_W - ROPE_HALF, 1)
        dn = pltpu.roll(t, ROPE_HALF, 1)
        return t * c_tab + up * s_up + dn * s_dn

    def proj(col):
        return jnp.dot(xb, wq_sc[:, col:col + MXU_WIDTH], preferred_element_type=F32)

    ones_row = (lax.broadcasted_iota(jnp.int32, (BF16_SUBLANES, tm), 0) == 0).astype(BF16)

    for c in range(width // MXU_WIDTH):
        q2 = proj(c * MXU_WIDTH)
        k2 = proj(width + c * MXU_WIDTH)
        v2 = proj(2 * width + c * MXU_WIDTH)
        for half in range(MXU_WIDTH // HEAD_W):
            head = c * (MXU_WIDTH // HEAD_W) + half
            lo = head * HEAD_W
            cols = slice(half * HEAD_W, (half + 1) * HEAD_W)
            qh_t = (rope(q2[:, cols]) * Q_SCALE).T
            qt_ref[0, lo:lo + HEAD_W, :] = jnp.where(first_t, qh_t, 0.0).astype(BF16)
            qt_ref[1, lo:lo + HEAD_W, :] = jnp.where(first_t, 0.0, qh_t).astype(BF16)
            k_ref[:, lo:lo + HEAD_W] = rope(k2[:, cols]).astype(BF16)
            vlo = head * VT_ROWS
            vt_ref[vlo:vlo + HEAD_W, :] = v2[:, cols].T.astype(BF16)
            vt_ref[vlo + HEAD_W:vlo + VT_ROWS, :] = ones_row


def _qkv_call(x2, w_qkv, inv_lane, later_weights, *, tm):
    s, d = x2.shape
    width = w_qkv.shape[2] // 3
    n_steps = s // tm
    cast_specs = [_cast_spec(w, n_steps) for w in later_weights]
    outs = pl.pallas_call(
        functools.partial(_qkv_kernel, tm=tm, width=width, n_cast=len(later_weights)),
        grid=(n_steps,),
        in_specs=[
            pl.BlockSpec((tm, d), lambda i: (i, 0)),
            _resident_layer(w_qkv, 0),
            _resident(inv_lane.shape),
            *cast_specs,
        ],
        out_specs=[
            pl.BlockSpec((2, width, tm), lambda i: (0, 0, i)),
            pl.BlockSpec((tm, width), lambda i: (i, 0)),
            pl.BlockSpec((N_HEADS * VT_ROWS, tm), lambda i: (0, i)),
            *cast_specs,
        ],
        out_shape=[
            jax.ShapeDtypeStruct((2, width, s), BF16),
            jax.ShapeDtypeStruct((s, width), BF16),
            jax.ShapeDtypeStruct((N_HEADS * VT_ROWS, s), BF16),
            *[jax.ShapeDtypeStruct(w.shape, BF16) for w in later_weights],
        ],
        scratch_shapes=[pltpu.VMEM((tm, HEAD_W), F32), pltpu.VMEM((tm, HEAD_W), F32),
                        pltpu.VMEM(w_qkv.shape[1:], BF16)],
        compiler_params=pltpu.CompilerParams(
            dimension_semantics=("arbitrary",), vmem_limit_bytes=V7X_VMEM_LIMIT_BYTES),
        name="qkv_rope",
    )(x2, w_qkv, inv_lane, *later_weights)
    return outs[:3], outs[3:]


def _attn_kernel(lam_ref, g_ref, qt_hbm, k_ref, vt_ref, o_ref, m_sc, acc_sc, s_sc, p_sc, a_sc, smax_sc,
                 q_sc, q_sem, *, tile, n_blocks, lam_init):
    n_pairs = n_blocks * (n_blocks + 1) // 2
    n_steps = n_pairs // 2 + 1

    m_sc[...] = jnp.full(m_sc.shape, NEG_INF, F32)
    acc_sc[...] = jnp.zeros(acc_sc.shape, F32)
    p_sc[...] = jnp.zeros(p_sc.shape, BF16)
    a_sc[...] = jnp.ones(a_sc.shape, F32)

    key = lax.broadcasted_iota(jnp.int32, (tile, tile), 0)
    qry = lax.broadcasted_iota(jnp.int32, (tile, tile), 1)
    diag_mask = key // CHUNK <= qry // CHUNK

    def clamped(x):
        return jnp.minimum(x, n_blocks - 1)

    head = pl.program_id(0)

    def q_copy(i):
        slot = i % Q_SLOTS
        src = qt_hbm.at[:, pl.ds(pl.multiple_of(head * HEAD_W, HEAD_W), HEAD_W),
                        pl.ds(pl.multiple_of(i * tile, tile), tile)]
        return pltpu.make_async_copy(src, q_sc.at[slot], q_sem.at[slot])

    def scores(i, t):
        k_t = k_ref[pl.ds(pl.multiple_of(t * tile, tile), tile), :]
        for c in range(2):
            s = jnp.dot(k_t, q_sc[i % Q_SLOTS, c], preferred_element_type=F32)
            s_sc[c] = s
            smax_sc[c] = jnp.max(s, axis=0, keepdims=True)

    def softmax(slot, t, masked):
        for c in range(2):
            s = s_sc[c]
            if masked:
                s = jnp.where(diag_mask, s, NEG_INF)
                s_max = jnp.max(s, axis=0, keepdims=True)
            else:
                s_max = smax_sc[c]
            m_prev = jnp.where(t == 0, NEG_INF, m_sc[c])
            m_new = jnp.maximum(m_prev, s_max)
            p_sc[slot, c] = jnp.exp2(s - m_new).astype(BF16)
            a_sc[slot, c] = jnp.exp2(m_prev - m_new)
            m_sc[c] = m_new

    def accumulate(slot, i, t):
        v_t = vt_ref[:, pl.ds(pl.multiple_of(clamped(t) * tile, tile), tile)]
        par = i % 2
        for c in range(2):
            acc_sc[par, c] = a_sc[slot, c] * acc_sc[par, c] + jnp.dot(
                v_t, p_sc[slot, c], preferred_element_type=F32)

    def finalize(i):
        lam_p = lam_ref[...]
        lam = (jnp.exp(jnp.sum(lam_p[0:1] * lam_p[1:2], axis=1, keepdims=True))
               - jnp.exp(jnp.sum(lam_p[2:3] * lam_p[3:4], axis=1, keepdims=True)) + lam_init)
        par = i % 2

        def normalised(c):
            return acc_sc[par, c, :HEAD_W] / acc_sc[par, c, HEAD_W:HEAD_W + 1]

        o = (normalised(0) - lam * normalised(1)).T
        o = o * lax.rsqrt(jnp.mean(o * o, axis=-1, keepdims=True) + LN_EPS) * g_ref[...]
        o_ref[pl.ds(pl.multiple_of(i * tile, tile), tile), :] = (o * (1.0 - lam_init)).astype(o_ref.dtype)

    def following(i, t):
        last = t == i
        return jnp.where(last, i + 1, i), jnp.where(last, 0, t + 1)

    q_copy(0).start()
    q_copy(0).wait()
    if n_blocks > 1:
        q_copy(1).start()
    scores(0, 0)

    def step(_, carry):
        i3, t3, i2, t2, i1, t1, i0, t0, done = carry

        @pl.when(done >= 0)
        def _():
            finalize(done)

        opened = jnp.where(t1 == 0, i1, jnp.where(t0 == 0, i0, -1))

        @pl.when((opened >= 1) & (opened < n_blocks))
        def _():
            q_copy(opened).wait()

        @pl.when((opened >= 1) & (opened + 1 < n_blocks))
        def _():
            q_copy(opened + 1).start()

        def stages(diag_a, diag_b):
            accumulate(1, i3, t3)
            softmax(0, t2, diag_a)
            scores(clamped(i1), clamped(t1))
            accumulate(0, i2, t2)
            softmax(1, t1, diag_b)
            scores(clamped(i0), clamped(t0))

        variant = jnp.where(t2 == i2, 1, jnp.where(t1 == i1, 2, 0))
        lax.switch(variant, [lambda: stages(False, False), lambda: stages(True, False),
                             lambda: stages(False, True)])
        done = jnp.where(t3 == i3, i3, jnp.where(t2 == i2, i2, -1))
        done = jnp.where(done < n_blocks, done, -1)
        nxt = following(i0, t0)
        return (i1, t1, i0, t0) + nxt + following(*nxt) + (done,)

    zero = jnp.int32(0)
    first = (zero, zero)
    second = following(*first)
    carry = lax.fori_loop(0, n_steps, step, first + first + second + following(*second) + (jnp.int32(-1),))

    @pl.when(carry[-1] >= 0)
    def _():
        finalize(carry[-1])


def _attn_call(lam_p, g_sub, qt, k, vt, *, tile, lam_init):
    s, width = k.shape
    return pl.pallas_call(
        functools.partial(_attn_kernel, tile=tile, n_blocks=s // tile, lam_init=lam_init),
        grid=(N_HEADS,),
        in_specs=[
            pl.BlockSpec(lam_p.shape, lambda h: (0, 0)),
            pl.BlockSpec(g_sub.shape, lambda h: (0, 0)),
            pl.BlockSpec(memory_space=pl.ANY),
            pl.BlockSpec((s, HEAD_W), lambda h: (0, h)),
            pl.BlockSpec((VT_ROWS, s), lambda h: (h, 0)),
        ],
        out_specs=pl.BlockSpec((s, HEAD_W), lambda h: (0, h)),
        out_shape=jax.ShapeDtypeStruct((s, width), BF16),
        scratch_shapes=[
            pltpu.VMEM((2, 1, tile), F32),
            pltpu.VMEM((2, 2, VT_ROWS, tile), F32),
            pltpu.VMEM((2, tile, tile), F32),
            pltpu.VMEM((2, 2, tile, tile), BF16),
            pltpu.VMEM((2, 2, 1, tile), F32),
            pltpu.VMEM((2, 1, tile), F32),
            pltpu.VMEM((Q_SLOTS, 2, HEAD_W, tile), BF16),
            pltpu.SemaphoreType.DMA((Q_SLOTS,)),
        ],
        compiler_params=pltpu.CompilerParams(
            dimension_semantics=("arbitrary",), vmem_limit_bytes=V7X_VMEM_LIMIT_BYTES),
        name="diff_attn",
    )(lam_p, g_sub, qt, k, vt)


def _swiglu(h, wg_ref, wu_ref, wd_ref, act_sc, rows):
    hb = h.astype(BF16)
    d_ff = wg_ref.shape[1]
    for c in range(d_ff // MXU_WIDTH):
        cols = slice(c * MXU_WIDTH, (c + 1) * MXU_WIDTH)
        gate = jnp.dot(hb, wg_ref[:, cols], preferred_element_type=F32)
        up = jnp.dot(hb, wu_ref[:, cols], preferred_element_type=F32)
        act_sc[rows, cols] = (gate / (1.0 + jnp.exp(-gate)) * up).astype(BF16)
    return jnp.dot(act_sc[rows, :], wd_ref[...], preferred_element_type=F32)


def _row_groups(tm):
    rows = tm // ROW_GROUPS
    return [slice(g * rows, (g + 1) * rows) for g in range(ROW_GROUPS)]


def _mixer_out_ffn(resid, mixed, w_ref, wg_ref, wu_ref, wd_ref, ln, out_ref, act_sc, groups):
    m = [jnp.dot(mixed(r), w_ref[...], preferred_element_type=F32) for r in groups]
    h = [_layer_norm(ALPHA * resid(r) + m_g, ln[0:1], ln[1:2]) for r, m_g in zip(groups, m)]
    y = [_swiglu(h_g, wg_ref, wu_ref, wd_ref, act_sc, r) for r, h_g in zip(groups, h)]
    for r, h_g, y_g in zip(groups, h, y):
        out_ref[r, :] = _layer_norm(ALPHA * h_g + y_g, ln[2:3], ln[3:4])


def _post_attn_kernel(x_ref, o_ref, wo_ref, wg_ref, wu_ref, wd_ref, ln_ref, out_ref, act_sc):
    ln = ln_ref[...]
    _mixer_out_ffn(lambda r: x_ref[r, :], lambda r: o_ref[r, :], wo_ref, wg_ref, wu_ref, wd_ref, ln,
                   out_ref, act_sc, _row_groups(x_ref.shape[0]))


def _post_attn_call(x2, o, w_o, w_gate, w_up, w_down, ln, *, tm, layer):
    s, d = x2.shape
    d_ff = w_gate.shape[2]
    return pl.pallas_call(
        _post_attn_kernel,
        grid=(s // tm,),
        in_specs=[
            pl.BlockSpec((tm, d), lambda i: (i, 0)),
            pl.BlockSpec((tm, o.shape[1]), lambda i: (i, 0)),
            _resident_layer(w_o, 0),
            _resident_layer(w_gate, layer),
            _resident_layer(w_up, layer),
            _resident_layer(w_down, layer),
            _resident(ln.shape),
        ],
        out_specs=pl.BlockSpec((tm, d), lambda i: (i, 0)),
        out_shape=jax.ShapeDtypeStruct((s, d), F32),
        scratch_shapes=[pltpu.VMEM((tm, d_ff), BF16)],
        compiler_params=pltpu.CompilerParams(
            dimension_semantics=("arbitrary",), vmem_limit_bytes=V7X_VMEM_LIMIT_BYTES),
        name="attn_out_ffn",
    )(x2, o, w_o, w_gate, w_up, w_down, ln)


def _conv_layer_kernel(h_ref, win_ref, wc_ref, wout_ref, wg_ref, wu_ref, wd_ref, ln_ref, out_ref,
                       act_sc, tail_sc, *, tm, ch):
    i = pl.program_id(0)
    ln = ln_ref[...]
    h = h_ref[...]
    hb = h.astype(BF16)

    @pl.when(i == 0)
    def _():
        tail_sc[...] = jnp.zeros(tail_sc.shape, F32)

    b_gate = jnp.dot(hb, win_ref[:, 0:ch], preferred_element_type=F32)
    c_gate = jnp.dot(hb, win_ref[:, ch:2 * ch], preferred_element_type=F32)
    xt = jnp.dot(hb, win_ref[:, 2 * ch:3 * ch], preferred_element_type=F32)
    u = c_gate * xt

    row = lax.broadcasted_iota(jnp.int32, (tm, ch), 0)
    tail = tail_sc[...]
    wc = wc_ref[...]
    y = wc[CONV_WIDTH - 1:CONV_WIDTH] * u
    for d in range(1, CONV_WIDTH):
        shifted = pltpu.roll(u, d, 0)
        for t in range(d):
            src = CONV_WIDTH - 1 - d + t
            shifted = jnp.where(row == t, tail[src:src + 1], shifted)
        y = y + wc[CONV_WIDTH - 1 - d:CONV_WIDTH - d] * shifted
    tail_sc[...] = u[tm - (CONV_WIDTH - 1):]

    mixed = (b_gate * y).astype(BF16)
    _mixer_out_ffn(lambda r: h[r, :], lambda r: mixed[r, :], wout_ref, wg_ref, wu_ref, wd_ref, ln,
                   out_ref, act_sc, _row_groups(tm))


def _conv_layer_call(h, w_in, w_conv, w_out, w_gate, w_up, w_down, ln, *, tm, layer):
    s, d = h.shape
    ch = w_out.shape[1]
    d_ff = w_gate.shape[2]
    return pl.pallas_call(
        functools.partial(_conv_layer_kernel, tm=tm, ch=ch),
        grid=(s // tm,),
        in_specs=[
            pl.BlockSpec((tm, d), lambda i: (i, 0)),
            _resident_layer(w_in, 0),
            _resident(w_conv.shape),
            _resident_layer(w_out, 0),
            _resident_layer(w_gate, layer),
            _resident_layer(w_up, layer),
            _resident_layer(w_down, layer),
            _resident(ln.shape),
        ],
        out_specs=pl.BlockSpec((tm, d), lambda i: (i, 0)),
        out_shape=jax.ShapeDtypeStruct((s, d), F32),
        scratch_shapes=[pltpu.VMEM((tm, d_ff), BF16), pltpu.VMEM((CONV_WIDTH - 1, ch), F32)],
        compiler_params=pltpu.CompilerParams(
            dimension_semantics=("arbitrary",), vmem_limit_bytes=V7X_VMEM_LIMIT_BYTES),
        name="conv_ffn",
    )(h, w_in, w_conv, w_out, w_gate, w_up, w_down, ln)


def _rope_inv_lanes():
    inv = ROPE_THETA ** (-jnp.arange(0, ROPE_DIM, 2, dtype=F32) / ROPE_DIM)
    per_comp = jnp.concatenate([inv, inv, jnp.zeros((HEAD_DIM - ROPE_DIM,), F32)])
    return jnp.tile(per_comp, HEAD_W // HEAD_DIM)[None, :]


def kernel(x, attn_w_qkv, attn_w_o, attn_lambda_q1, attn_lambda_k1, attn_lambda_q2, attn_lambda_k2,
           attn_subln_g, conv_w_in, conv_w, conv_w_out, ffn_w_gate, ffn_w_up, ffn_w_down,
           ln1_g, ln1_b, ln2_g, ln2_b):
    b, s, d = x.shape
    assert b == 1 and attn_w_qkv.shape[0] == 1 and conv_w_in.shape[0] == 1
    tm = min(512, s)
    tile = min(1024, s)
    x2 = x.reshape(s, d)

    later = (attn_w_o, conv_w_in, conv_w_out, ffn_w_gate, ffn_w_up, ffn_w_down)
    (qt, k, vt), (w_o, w_in, w_out, w_gate, w_up, w_down) = _qkv_call(
        x2, attn_w_qkv, _rope_inv_lanes(), later, tm=tm)
    lam_p = jnp.concatenate([attn_lambda_q1, attn_lambda_k1, attn_lambda_q2, attn_lambda_k2], axis=0)
    o = _attn_call(lam_p, attn_subln_g, qt, k, vt, tile=tile, lam_init=LAMBDA_INIT_0)

    ln0 = jnp.stack([ln1_g[0], ln1_b[0], ln2_g[0], ln2_b[0]])
    h = _post_attn_call(x2, o, w_o, w_gate, w_up, w_down, ln0, tm=tm, layer=0)

    ln1 = jnp.stack([ln1_g[1], ln1_b[1], ln2_g[1], ln2_b[1]])
    out = _conv_layer_call(h, w_in, conv_w[0], w_out, w_gate, w_up, w_down, ln1, tm=tm, layer=1)
    return out.reshape(b, s, d)
```

```python
import functools
import math

import jax
import jax.numpy as jnp
from jax import lax
from jax.experimental import pallas as pl
from jax.experimental.pallas import tpu as pltpu

F32 = jnp.float32
BF16 = jnp.bfloat16

CHUNK = 64
N_HEADS = 8
HEAD_DIM = 64
HEAD_W = 2 * HEAD_DIM
ROPE_DIM = HEAD_DIM // 4
ROPE_HALF = ROPE_DIM // 2
ROPE_THETA = 500000.0
CONV_WIDTH = 3
DEPTH = 2
ALPHA = (2 * DEPTH) ** 0.25
LN_EPS = 1e-5
NEG_INF = -1e30
LAMBDA_INIT_0 = 0.8 - 0.6 * math.exp(-0.3 * 0)

V7X_VMEM_LIMIT_BYTES = 56 * 1024 * 1024
MXU_WIDTH = 256
ROW_GROUPS = 2
BF16_SUBLANES = 16
VT_ROWS = HEAD_W + BF16_SUBLANES
Q_SCALE = HEAD_DIM ** -0.5 * math.log2(math.e)


def _resident(shape):
    return pl.BlockSpec(shape, lambda *_: (0,) * len(shape), pipeline_mode=pl.Buffered(1))


def _resident_layer(stacked, layer):
    return pl.BlockSpec((None,) + stacked.shape[1:], lambda *_: (layer, 0, 0), pipeline_mode=pl.Buffered(1))


def _cast_spec(stacked, n_steps):
    rows = stacked.shape[1]
    n_blk = max(d for d in range(1, n_steps + 1) if rows % d == 0 and (rows // d) % BF16_SUBLANES == 0)
    block = (stacked.shape[0], rows // n_blk, stacked.shape[2])
    return pl.BlockSpec(block, lambda i: (0, i * n_blk // n_steps, 0))


def _layer_norm(x, g, b):
    mu = jnp.mean(x, axis=-1, keepdims=True)
    xc = x - mu
    var = jnp.mean(xc * xc, axis=-1, keepdims=True)
    return xc * lax.rsqrt(var + LN_EPS) * g + b


def _qkv_kernel(x_ref, w_ref, inv_ref, *rest, tm, width, n_cast):
    cast_src, (qt_ref, k_ref, vt_ref), cast_dst, (cos_sc, sin_sc, wq_sc) = (
        rest[:n_cast], rest[n_cast:n_cast + 3], rest[n_cast + 3:2 * n_cast + 3], rest[2 * n_cast + 3:])
    i = pl.program_id(0)
    xb = x_ref[...].astype(BF16)
    inv = inv_ref[...]

    for src, dst in zip(cast_src, cast_dst):
        dst[...] = src[...].astype(BF16)

    @pl.when(i == 0)
    def _():
        wq_sc[...] = w_ref[...].astype(BF16)
        off = lax.broadcasted_iota(jnp.int32, (tm, HEAD_W), 0).astype(F32) * inv
        cos_sc[...] = jnp.cos(off)
        sin_sc[...] = jnp.sin(off)

    base = (i * tm).astype(F32) * inv
    cos_b, sin_b = jnp.cos(base), jnp.sin(base)
    cos_r, sin_r = cos_sc[...], sin_sc[...]
    cos = cos_b * cos_r - sin_b * sin_r
    sin = sin_b * cos_r + cos_b * sin_r
    lane = lax.broadcasted_iota(jnp.int32, (tm, HEAD_W), 1)
    w = lane % HEAD_DIM
    c_tab = jnp.where(w < ROPE_DIM, cos, 1.0)
    s_up = jnp.where(w < ROPE_HALF, -sin, 0.0)
    s_dn = jnp.where((w >= ROPE_HALF) & (w < ROPE_DIM), sin, 0.0)
    first_t = lax.broadcasted_iota(jnp.int32, (HEAD_W, tm), 0) < HEAD_DIM

    def rope(t):
        up = pltpu.roll(t, HEAD_W - ROPE_HALF, 1)
        dn = pltpu.roll(t, ROPE_HALF, 1)
        return t * c_tab + up * s_up + dn * s_dn

    def proj(col):
        return jnp.dot(xb, wq_sc[:, col:col + MXU_WIDTH], preferred_element_type=F32)

    ones_row = (lax.broadcasted_iota(jnp.int32, (BF16_SUBLANES, tm), 0) == 0).astype(BF16)

    for c in range(width // MXU_WIDTH):
        q2 = proj(c * MXU_WIDTH)
        k2 = proj(width + c * MXU_WIDTH)
        v2 = proj(2 * width + c * MXU_WIDTH)
        for half in range(MXU_WIDTH // HEAD_W):
            head = c * (MXU_WIDTH // HEAD_W) + half
            lo = head * HEAD_W
            cols = slice(half * HEAD_W, (half + 1) * HEAD_W)
            qh_t = (rope(q2[:, cols]) * Q_SCALE).T
            qt_ref[0, lo:lo + HEAD_W, :] = jnp.where(first_t, qh_t, 0.0).astype(BF16)
            qt_ref[1, lo:lo + HEAD_W, :] = jnp.where(first_t, 0.0, qh_t).astype(BF16)
            k_ref[:, lo:lo + HEAD_W] = rope(k2[:, cols]).astype(BF16)
            vlo = head * VT_ROWS
            vt_ref[vlo:vlo + HEAD_W, :] = v2[:, cols].T.astype(BF16)
            vt_ref[vlo + HEAD_W:vlo + VT_ROWS, :] = ones_row


def _qkv_call(x2, w_qkv, inv_lane, later_weights, *, tm):
    s, d = x2.shape
    width = w_qkv.shape[2] // 3
    n_steps = s // tm
    cast_specs = [_cast_spec(w, n_steps) for w in later_weights]
    outs = pl.pallas_call(
        functools.partial(_qkv_kernel, tm=tm, width=width, n_cast=len(later_weights)),
        grid=(n_steps,),
        in_specs=[
            pl.BlockSpec((tm, d), lambda i: (i, 0)),
            _resident_layer(w_qkv, 0),
            _resident(inv_lane.shape),
            *cast_specs,
        ],
        out_specs=[
            pl.BlockSpec((2, width, tm), lambda i: (0, 0, i)),
            pl.BlockSpec((tm, width), lambda i: (i, 0)),
            pl.BlockSpec((N_HEADS * VT_ROWS, tm), lambda i: (0, i)),
            *cast_specs,
        ],
        out_shape=[
            jax.ShapeDtypeStruct((2, width, s), BF16),
            jax.ShapeDtypeStruct((s, width), BF16),
            jax.ShapeDtypeStruct((N_HEADS * VT_ROWS, s), BF16),
            *[jax.ShapeDtypeStruct(w.shape, BF16) for w in later_weights],
        ],
        scratch_shapes=[pltpu.VMEM((tm, HEAD_W), F32), pltpu.VMEM((tm, HEAD_W), F32),
                        pltpu.VMEM(w_qkv.shape[1:], BF16)],
        compiler_params=pltpu.CompilerParams(
            dimension_semantics=("arbitrary",), vmem_limit_bytes=V7X_VMEM_LIMIT_BYTES),
        name="qkv_rope",
    )(x2, w_qkv, inv_lane, *later_weights)
    return outs[:3], outs[3:]


def _attn_kernel(lam_ref, g_ref, qt_ref, k_ref, vt_ref, o_ref, m_sc, acc_sc, s_sc, p_sc, a_sc, smax_sc,
                 *, tile, n_blocks, lam_init):
    n_pairs = n_blocks * (n_blocks + 1) // 2
    n_steps = n_pairs // 2 + 1

    m_sc[...] = jnp.full(m_sc.shape, NEG_INF, F32)
    acc_sc[...] = jnp.zeros(acc_sc.shape, F32)
    p_sc[...] = jnp.zeros(p_sc.shape, BF16)
    a_sc[...] = jnp.ones(a_sc.shape, F32)

    key = lax.broadcasted_iota(jnp.int32, (tile, tile), 0)
    qry = lax.broadcasted_iota(jnp.int32, (tile, tile), 1)
    diag_mask = key // CHUNK <= qry // CHUNK

    def clamped(x):
        return jnp.minimum(x, n_blocks - 1)

    def scores(i, t):
        k_t = k_ref[pl.ds(pl.multiple_of(t * tile, tile), tile), :]
        q_lo = pl.multiple_of(i * tile, tile)
        for c in range(2):
            s = jnp.dot(k_t, qt_ref[c, :, pl.ds(q_lo, tile)], preferred_element_type=F32)
            s_sc[c] = s
            smax_sc[c] = jnp.max(s, axis=0, keepdims=True)

    def softmax(slot, t, masked):
        for c in range(2):
            s = s_sc[c]
            if masked:
                s = jnp.where(diag_mask, s, NEG_INF)
                s_max = jnp.max(s, axis=0, keepdims=True)
            else:
                s_max = smax_sc[c]
            m_prev = jnp.where(t == 0, NEG_INF, m_sc[c])
            m_new = jnp.maximum(m_prev, s_max)
            p_sc[slot, c] = jnp.exp2(s - m_new).astype(BF16)
            a_sc[slot, c] = jnp.exp2(m_prev - m_new)
            m_sc[c] = m_new

    def accumulate(slot, i, t):
        v_t = vt_ref[:, pl.ds(pl.multiple_of(clamped(t) * tile, tile), tile)]
        par = i % 2
        for c in range(2):
            acc_sc[par, c] = a_sc[slot, c] * acc_sc[par, c] + jnp.dot(
                v_t, p_sc[slot, c], preferred_element_type=F32)

    def finalize(i):
        lam_p = lam_ref[...]
        lam = (jnp.exp(jnp.sum(lam_p[0:1] * lam_p[1:2], axis=1, keepdims=True))
               - jnp.exp(jnp.sum(lam_p[2:3] * lam_p[3:4], axis=1, keepdims=True)) + lam_init)
        par = i % 2

        def normalised(c):
            return acc_sc[par, c, :HEAD_W] / acc_sc[par, c, HEAD_W:HEAD_W + 1]

        o_t = normalised(0) - lam * normalised(1)
        o_t = o_t * lax.rsqrt(jnp.mean(o_t * o_t, axis=0, keepdims=True) + LN_EPS) * g_ref[...]
        o_ref[:, pl.ds(pl.multiple_of(i * tile, tile), tile)] = (o_t * (1.0 - lam_init)).astype(o_ref.dtype)

    def following(i, t):
        last = t == i
        return jnp.where(last, i + 1, i), jnp.where(last, 0, t + 1)

    scores(0, 0)

    def step(_, carry):
        i3, t3, i2, t2, i1, t1, i0, t0, done = carry

        @pl.when(done >= 0)
        def _():
            finalize(done)

        def stages(diag_a, diag_b):
            accumulate(1, i3, t3)
            softmax(0, t2, diag_a)
            scores(clamped(i1), clamped(t1))
            accumulate(0, i2, t2)
            softmax(1, t1, diag_b)
            scores(clamped(i0), clamped(t0))

        variant = jnp.where(t2 == i2, 1, jnp.where(t1 == i1, 2, 0))
        lax.switch(variant, [lambda: stages(False, False), lambda: stages(True, False),
                             lambda: stages(False, True)])
        done = jnp.where(t3 == i3, i3, jnp.where(t2 == i2, i2, -1))
        done = jnp.where(done < n_blocks, done, -1)
        nxt = following(i0, t0)
        return (i1, t1, i0, t0) + nxt + following(*nxt) + (done,)

    zero = jnp.int32(0)
    first = (zero, zero)
    second = following(*first)
    carry = lax.fori_loop(0, n_steps, step, first + first + second + following(*second) + (jnp.int32(-1),))

    @pl.when(carry[-1] >= 0)
    def _():
        finalize(carry[-1])


def _attn_call(lam_p, g_sub, qt, k, vt, *, tile, lam_init):
    s, width = k.shape
    return pl.pallas_call(
        functools.partial(_attn_kernel, tile=tile, n_blocks=s // tile, lam_init=lam_init),
        grid=(N_HEADS,),
        in_specs=[
            pl.BlockSpec(lam_p.shape, lambda h: (0, 0)),
            pl.BlockSpec(g_sub.shape, lambda h: (0, 0)),
            pl.BlockSpec((2, HEAD_W, s), lambda h: (0, h, 0), pipeline_mode=pl.Buffered(1)),
            pl.BlockSpec((s, HEAD_W), lambda h: (0, h)),
            pl.BlockSpec((VT_ROWS, s), lambda h: (h, 0), pipeline_mode=pl.Buffered(1)),
        ],
        out_specs=pl.BlockSpec((HEAD_W, s), lambda h: (h, 0)),
        out_shape=jax.ShapeDtypeStruct((width, s), BF16),
        scratch_shapes=[
            pltpu.VMEM((2, 1, tile), F32),
            pltpu.VMEM((2, 2, VT_ROWS, tile), F32),
            pltpu.VMEM((2, tile, tile), F32),
            pltpu.VMEM((2, 2, tile, tile), BF16),
            pltpu.VMEM((2, 2, 1, tile), F32),
            pltpu.VMEM((2, 1, tile), F32),
        ],
        compiler_params=pltpu.CompilerParams(
            dimension_semantics=("arbitrary",), vmem_limit_bytes=V7X_VMEM_LIMIT_BYTES),
        name="diff_attn",
    )(lam_p, g_sub, qt, k, vt)


def _swiglu(h, wg_ref, wu_ref, wd_ref, act_sc, rows):
    hb = h.astype(BF16)
    d_ff = wg_ref.shape[1]
    for c in range(d_ff // MXU_WIDTH):
        cols = slice(c * MXU_WIDTH, (c + 1) * MXU_WIDTH)
        gate = jnp.dot(hb, wg_ref[:, cols], preferred_element_type=F32)
        up = jnp.dot(hb, wu_ref[:, cols], preferred_element_type=F32)
        act_sc[rows, cols] = (gate / (1.0 + jnp.exp(-gate)) * up).astype(BF16)
    return jnp.dot(act_sc[rows, :], wd_ref[...], preferred_element_type=F32)


def _row_groups(tm):
    rows = tm // ROW_GROUPS
    return [slice(g * rows, (g + 1) * rows) for g in range(ROW_GROUPS)]


def _mixer_out_ffn(resid, project, wg_ref, wu_ref, wd_ref, ln, out_ref, act_sc, groups):
    m = [project(r) for r in groups]
    h = [_layer_norm(ALPHA * resid(r) + m_g, ln[0:1], ln[1:2]) for r, m_g in zip(groups, m)]
    y = [_swiglu(h_g, wg_ref, wu_ref, wd_ref, act_sc, r) for r, h_g in zip(groups, h)]
    for r, h_g, y_g in zip(groups, h, y):
        out_ref[r, :] = _layer_norm(ALPHA * h_g + y_g, ln[2:3], ln[3:4])


def _post_attn_kernel(x_ref, o_ref, wo_ref, wg_ref, wu_ref, wd_ref, ln_ref, out_ref, act_sc):
    ln = ln_ref[...]
    def project(r):
        return lax.dot_general(o_ref[:, r], wo_ref[...], (((0,), (0,)), ((), ())), preferred_element_type=F32)

    _mixer_out_ffn(lambda r: x_ref[r, :], project, wg_ref, wu_ref, wd_ref, ln,
                   out_ref, act_sc, _row_groups(x_ref.shape[0]))


def _post_attn_call(x2, o, w_o, w_gate, w_up, w_down, ln, *, tm, layer):
    s, d = x2.shape
    d_ff = w_gate.shape[2]
    return pl.pallas_call(
        _post_attn_kernel,
        grid=(s // tm,),
        in_specs=[
            pl.BlockSpec((tm, d), lambda i: (i, 0)),
            pl.BlockSpec((o.shape[0], tm), lambda i: (0, i)),
            _resident_layer(w_o, 0),
            _resident_layer(w_gate, layer),
            _resident_layer(w_up, layer),
            _resident_layer(w_down, layer),
            _resident(ln.shape),
        ],
        out_specs=pl.BlockSpec((tm, d), lambda i: (i, 0)),
        out_shape=jax.ShapeDtypeStruct((s, d), F32),
        scratch_shapes=[pltpu.VMEM((tm, d_ff), BF16)],
        compiler_params=pltpu.CompilerParams(
            dimension_semantics=("arbitrary",), vmem_limit_bytes=V7X_VMEM_LIMIT_BYTES),
        name="attn_out_ffn",
    )(x2, o, w_o, w_gate, w_up, w_down, ln)


def _conv_layer_kernel(h_ref, win_ref, wc_ref, wout_ref, wg_ref, wu_ref, wd_ref, ln_ref, out_ref,
                       act_sc, tail_sc, *, tm, ch):
    i = pl.program_id(0)
    ln = ln_ref[...]
    h = h_ref[...]
    hb = h.astype(BF16)

    @pl.when(i == 0)
    def _():
        tail_sc[...] = jnp.zeros(tail_sc.shape, F32)

    b_gate = jnp.dot(hb, win_ref[:, 0:ch], preferred_element_type=F32)
    c_gate = jnp.dot(hb, win_ref[:, ch:2 * ch], preferred_element_type=F32)
    xt = jnp.dot(hb, win_ref[:, 2 * ch:3 * ch], preferred_element_type=F32)
    u = c_gate * xt

    row = lax.broadcasted_iota(jnp.int32, (tm, ch), 0)
    tail = tail_sc[...]
    wc = wc_ref[...]
    y = wc[CONV_WIDTH - 1:CONV_WIDTH] * u
    for d in range(1, CONV_WIDTH):
        shifted = pltpu.roll(u, d, 0)
        for t in range(d):
            src = CONV_WIDTH - 1 - d + t
            shifted = jnp.where(row == t, tail[src:src + 1], shifted)
        y = y + wc[CONV_WIDTH - 1 - d:CONV_WIDTH - d] * shifted
    tail_sc[...] = u[tm - (CONV_WIDTH - 1):]

    mixed = (b_gate * y).astype(BF16)
    def project(r):
        return jnp.dot(mixed[r, :], wout_ref[...], preferred_element_type=F32)

    _mixer_out_ffn(lambda r: h[r, :], project, wg_ref, wu_ref, wd_ref, ln,
                   out_ref, act_sc, _row_groups(tm))


def _conv_layer_call(h, w_in, w_conv, w_out, w_gate, w_up, w_down, ln, *, tm, layer):
    s, d = h.shape
    ch = w_out.shape[1]
    d_ff = w_gate.shape[2]
    return pl.pallas_call(
        functools.partial(_conv_layer_kernel, tm=tm, ch=ch),
        grid=(s // tm,),
        in_specs=[
            pl.BlockSpec((tm, d), lambda i: (i, 0)),
            _resident_layer(w_in, 0),
            _resident(w_conv.shape),
            _resident_layer(w_out, 0),
            _resident_layer(w_gate, layer),
            _resident_layer(w_up, layer),
            _resident_layer(w_down, layer),
            _resident(ln.shape),
        ],
        out_specs=pl.BlockSpec((tm, d), lambda i: (i, 0)),
        out_shape=jax.ShapeDtypeStruct((s, d), F32),
        scratch_shapes=[pltpu.VMEM((tm, d_ff), BF16), pltpu.VMEM((CONV_WIDTH - 1, ch), F32)],
        compiler_params=pltpu.CompilerParams(
            dimension_semantics=("arbitrary",), vmem_limit_bytes=V7X_VMEM_LIMIT_BYTES),
        name="conv_ffn",
    )(h, w_in, w_conv, w_out, w_gate, w_up, w_down, ln)


def _rope_inv_lanes():
    inv = ROPE_THETA ** (-jnp.arange(0, ROPE_DIM, 2, dtype=F32) / ROPE_DIM)
    per_comp = jnp.concatenate([inv, inv, jnp.zeros((HEAD_DIM - ROPE_DIM,), F32)])
    return jnp.tile(per_comp, HEAD_W // HEAD_DIM)[None, :]


def kernel(x, attn_w_qkv, attn_w_o, attn_lambda_q1, attn_lambda_k1, attn_lambda_q2, attn_lambda_k2,
           attn_subln_g, conv_w_in, conv_w, conv_w_out, ffn_w_gate, ffn_w_up, ffn_w_down,
           ln1_g, ln1_b, ln2_g, ln2_b):
    b, s, d = x.shape
    assert b == 1 and attn_w_qkv.shape[0] == 1 and conv_w_in.shape[0] == 1
    tm = min(512, s)
    tile = min(1024, s)
    x2 = x.reshape(s, d)

    later = (attn_w_o, conv_w_in, conv_w_out, ffn_w_gate, ffn_w_up, ffn_w_down)
    (qt, k, vt), (w_o, w_in, w_out, w_gate, w_up, w_down) = _qkv_call(
        x2, attn_w_qkv, _rope_inv_lanes(), later, tm=tm)
    lam_p = jnp.concatenate([attn_lambda_q1, attn_lambda_k1, attn_lambda_q2, attn_lambda_k2], axis=0)
    o = _attn_call(lam_p, attn_subln_g.reshape(HEAD_W, 1), qt, k, vt, tile=tile, lam_init=LAMBDA_INIT_0)

    ln0 = jnp.stack([ln1_g[0], ln1_b[0], ln2_g[0], ln2_b[0]])
    h = _post_attn_call(x2, o, w_o, w_gate, w_up, w_down, ln0, tm=tm, layer=0)

    ln1 = jnp.stack([ln1_g[1], ln1_b[1], ln2_g[1], ln2_b[1]])
    out = _conv_layer_call(h, w_in, conv_w[0], w_out, w_gate, w_up, w_down, ln1, tm=tm, layer=1)
    return out.reshape(b, s, d)
```

```python
import functools
import math

import jax
import jax.numpy as jnp
from jax import lax
from jax.experimental import pallas as pl
from jax.experimental.pallas import tpu as pltpu

F32 = jnp.float32
BF16 = jnp.bfloat16

CHUNK = 64
N_HEADS = 8
HEAD_DIM = 64
HEAD_W = 2 * HEAD_DIM
ROPE_DIM = HEAD_DIM // 4
ROPE_HALF = ROPE_DIM // 2
ROPE_THETA = 500000.0
CONV_WIDTH = 3
DEPTH = 2
ALPHA = (2 * DEPTH) ** 0.25
LN_EPS = 1e-5
NEG_INF = -1e30
LAMBDA_INIT_0 = 0.8 - 0.6 * math.exp(-0.3 * 0)

V7X_VMEM_LIMIT_BYTES = 56 * 1024 * 1024
MXU_WIDTH = 256
LANES = 128
ROW_GROUPS = 2
BF16_SUBLANES = 16
VT_ROWS = HEAD_W + BF16_SUBLANES
Q_SCALE = HEAD_DIM ** -0.5 * math.log2(math.e)


def _resident(shape):
    return pl.BlockSpec(shape, lambda *_: (0,) * len(shape), pipeline_mode=pl.Buffered(1))


def _resident_layer(stacked, layer):
    return pl.BlockSpec((None,) + stacked.shape[1:], lambda *_: (layer, 0, 0), pipeline_mode=pl.Buffered(1))


def _cast_spec(stacked, n_steps):
    rows = stacked.shape[1]
    n_blk = max(d for d in range(1, n_steps + 1) if rows % d == 0 and (rows // d) % BF16_SUBLANES == 0)
    block = (stacked.shape[0], rows // n_blk, stacked.shape[2])
    return pl.BlockSpec(block, lambda i: (0, i * n_blk // n_steps, 0))


def _layer_norm(x, g, b):
    mu = jnp.mean(x, axis=-1, keepdims=True)
    xc = x - mu
    var = jnp.mean(xc * xc, axis=-1, keepdims=True)
    return xc * lax.rsqrt(var + LN_EPS) * g + b


def _qkv_kernel(x_ref, w_ref, inv_ref, *rest, tm, width, n_cast):
    cast_src, (q_ref, k_ref, vt_ref), cast_dst, (cos_sc, sin_sc, wq_sc) = (
        rest[:n_cast], rest[n_cast:n_cast + 3], rest[n_cast + 3:2 * n_cast + 3], rest[2 * n_cast + 3:])
    i = pl.program_id(0)
    xb = x_ref[...].astype(BF16)
    inv = inv_ref[...]

    for src, dst in zip(cast_src, cast_dst):
        dst[...] = src[...].astype(BF16)

    @pl.when(i == 0)
    def _():
        wq_sc[...] = w_ref[...].astype(BF16)
        off = lax.broadcasted_iota(jnp.int32, (tm, HEAD_W), 0).astype(F32) * inv
        cos_sc[...] = jnp.cos(off)
        sin_sc[...] = jnp.sin(off)

    base = (i * tm).astype(F32) * inv
    cos_b, sin_b = jnp.cos(base), jnp.sin(base)
    cos_r, sin_r = cos_sc[...], sin_sc[...]
    cos = cos_b * cos_r - sin_b * sin_r
    sin = sin_b * cos_r + cos_b * sin_r
    lane = lax.broadcasted_iota(jnp.int32, (tm, HEAD_W), 1)
    w = lane % HEAD_DIM
    c_tab = jnp.where(w < ROPE_DIM, cos, 1.0)
    s_up = jnp.where(w < ROPE_HALF, -sin, 0.0)
    s_dn = jnp.where((w >= ROPE_HALF) & (w < ROPE_DIM), sin, 0.0)
    first = lane < HEAD_DIM

    def rope(t):
        up = pltpu.roll(t, HEAD_W - ROPE_HALF, 1)
        dn = pltpu.roll(t, ROPE_HALF, 1)
        return t * c_tab + up * s_up + dn * s_dn

    def proj(col):
        return jnp.dot(xb, wq_sc[:, col:col + MXU_WIDTH], preferred_element_type=F32)

    ones_row = (lax.broadcasted_iota(jnp.int32, (BF16_SUBLANES, tm), 0) == 0).astype(BF16)

    for c in range(width // MXU_WIDTH):
        q2 = proj(c * MXU_WIDTH)
        k2 = proj(width + c * MXU_WIDTH)
        v2 = proj(2 * width + c * MXU_WIDTH)
        for half in range(MXU_WIDTH // HEAD_W):
            head = c * (MXU_WIDTH // HEAD_W) + half
            lo = head * HEAD_W
            cols = slice(half * HEAD_W, (half + 1) * HEAD_W)
            qh = rope(q2[:, cols]) * Q_SCALE
            q_ref[0, :, lo:lo + HEAD_W] = jnp.where(first, qh, 0.0).astype(BF16)
            q_ref[1, :, lo:lo + HEAD_W] = jnp.where(first, 0.0, qh).astype(BF16)
            k_ref[:, lo:lo + HEAD_W] = rope(k2[:, cols]).astype(BF16)
            vlo = head * VT_ROWS
            vt_ref[vlo:vlo + HEAD_W, :] = v2[:, cols].T.astype(BF16)
            vt_ref[vlo + HEAD_W:vlo + VT_ROWS, :] = ones_row


def _qkv_call(x2, w_qkv, inv_lane, later_weights, *, tm):
    s, d = x2.shape
    width = w_qkv.shape[2] // 3
    n_steps = s // tm
    cast_specs = [_cast_spec(w, n_steps) for w in later_weights]
    outs = pl.pallas_call(
        functools.partial(_qkv_kernel, tm=tm, width=width, n_cast=len(later_weights)),
        grid=(n_steps,),
        in_specs=[
            pl.BlockSpec((tm, d), lambda i: (i, 0)),
            _resident_layer(w_qkv, 0),
            _resident(inv_lane.shape),
            *cast_specs,
        ],
        out_specs=[
            pl.BlockSpec((2, tm, width), lambda i: (0, i, 0)),
            pl.BlockSpec((tm, width), lambda i: (i, 0)),
            pl.BlockSpec((N_HEADS * VT_ROWS, tm), lambda i: (0, i)),
            *cast_specs,
        ],
        out_shape=[
            jax.ShapeDtypeStruct((2, s, width), BF16),
            jax.ShapeDtypeStruct((s, width), BF16),
            jax.ShapeDtypeStruct((N_HEADS * VT_ROWS, s), BF16),
            *[jax.ShapeDtypeStruct(w.shape, BF16) for w in later_weights],
        ],
        scratch_shapes=[pltpu.VMEM((tm, HEAD_W), F32), pltpu.VMEM((tm, HEAD_W), F32),
                        pltpu.VMEM(w_qkv.shape[1:], BF16)],
        compiler_params=pltpu.CompilerParams(
            dimension_semantics=("arbitrary",), vmem_limit_bytes=V7X_VMEM_LIMIT_BYTES),
        name="qkv_rope",
    )(x2, w_qkv, inv_lane, *later_weights)
    return outs[:3], outs[3:]


def _attn_kernel(lam_ref, g_ref, q_ref, k_ref, vt_ref, o_ref, m_sc, acc_sc, s_sc, p_sc, a_sc, smax_sc,
                 *, tile, n_blocks, lam_init):
    n_pairs = n_blocks * (n_blocks + 1) // 2
    n_steps = n_pairs // 2 + 1

    m_sc[...] = jnp.full(m_sc.shape, NEG_INF, F32)
    acc_sc[...] = jnp.zeros(acc_sc.shape, F32)
    p_sc[...] = jnp.zeros(p_sc.shape, BF16)
    a_sc[...] = jnp.ones(a_sc.shape, F32)

    key = lax.broadcasted_iota(jnp.int32, (tile, tile), 0)
    qry = lax.broadcasted_iota(jnp.int32, (tile, tile), 1)
    diag_mask = key // CHUNK <= qry // CHUNK

    def clamped(x):
        return jnp.minimum(x, n_blocks - 1)

    def scores(i, t):
        k_t = k_ref[pl.ds(pl.multiple_of(t * tile, tile), tile), :]
        q_lo = pl.multiple_of(i * tile, tile)
        for c in range(2):
            s = lax.dot_general(k_t, q_ref[c, pl.ds(q_lo, tile), :], (((1,), (1,)), ((), ())),
                                preferred_element_type=F32)
            s_sc[c] = s
            smax_sc[c] = jnp.max(s, axis=0, keepdims=True)

    def softmax(slot, t, masked):
        for c in range(2):
            s = s_sc[c]
            if masked:
                s = jnp.where(diag_mask, s, NEG_INF)
                s_max = jnp.max(s, axis=0, keepdims=True)
            else:
                s_max = smax_sc[c]
            m_prev = jnp.where(t == 0, NEG_INF, m_sc[c])
            m_new = jnp.maximum(m_prev, s_max)
            p_sc[slot, c] = jnp.exp2(s - m_new).astype(BF16)
            a_sc[slot, c] = jnp.exp2(m_prev - m_new)
            m_sc[c] = m_new

    def accumulate(slot, i, t):
        v_t = vt_ref[:, pl.ds(pl.multiple_of(clamped(t) * tile, tile), tile)]
        par = i % 2
        for c in range(2):
            acc_sc[par, c] = a_sc[slot, c] * acc_sc[par, c] + jnp.dot(
                v_t, p_sc[slot, c], preferred_element_type=F32)

    def finalize(i):
        lam_p = lam_ref[...]
        lam = (jnp.exp(jnp.sum(lam_p[0:1] * lam_p[1:2], axis=1, keepdims=True))
               - jnp.exp(jnp.sum(lam_p[2:3] * lam_p[3:4], axis=1, keepdims=True)) + lam_init)
        par = i % 2

        g_b = jnp.broadcast_to(g_ref[...] * (1.0 - lam_init), (HEAD_W, LANES))
        for j in range(tile // LANES):
            cols = slice(j * LANES, (j + 1) * LANES)

            def normalised(c):
                return acc_sc[par, c, :HEAD_W, cols] / acc_sc[par, c, HEAD_W:HEAD_W + 1, cols]

            o_t = normalised(0) - lam * normalised(1)
            o_t = o_t * lax.rsqrt(jnp.mean(o_t * o_t, axis=0, keepdims=True) + LN_EPS) * g_b
            o_ref[:, pl.ds(pl.multiple_of(i * tile + j * LANES, LANES), LANES)] = o_t.astype(o_ref.dtype)

    def following(i, t):
        last = t == i
        return jnp.where(last, i + 1, i), jnp.where(last, 0, t + 1)

    scores(0, 0)

    def step(_, carry):
        i3, t3, i2, t2, i1, t1, i0, t0, done = carry

        @pl.when(done >= 0)
        def _():
            finalize(done)

        def stages(diag_a, diag_b):
            accumulate(1, i3, t3)
            softmax(0, t2, diag_a)
            scores(clamped(i1), clamped(t1))
            accumulate(0, i2, t2)
            softmax(1, t1, diag_b)
            scores(clamped(i0), clamped(t0))

        variant = jnp.where(t2 == i2, 1, jnp.where(t1 == i1, 2, 0))
        lax.switch(variant, [lambda: stages(False, False), lambda: stages(True, False),
                             lambda: stages(False, True)])
        done = jnp.where(t3 == i3, i3, jnp.where(t2 == i2, i2, -1))
        done = jnp.where(done < n_blocks, done, -1)
        nxt = following(i0, t0)
        return (i1, t1, i0, t0) + nxt + following(*nxt) + (done,)

    zero = jnp.int32(0)
    first = (zero, zero)
    second = following(*first)
    carry = lax.fori_loop(0, n_steps, step, first + first + second + following(*second) + (jnp.int32(-1),))

    @pl.when(carry[-1] >= 0)
    def _():
        finalize(carry[-1])


def _attn_call(lam_p, g_sub, q, k, vt, *, tile, lam_init):
    s, width = k.shape
    return pl.pallas_call(
        functools.partial(_attn_kernel, tile=tile, n_blocks=s // tile, lam_init=lam_init),
        grid=(N_HEADS,),
        in_specs=[
            pl.BlockSpec(lam_p.shape, lambda h: (0, 0)),
            pl.BlockSpec(g_sub.shape, lambda h: (0, 0)),
            pl.BlockSpec((2, s, HEAD_W), lambda h: (0, 0, h), pipeline_mode=pl.Buffered(1)),
            pl.BlockSpec((s, HEAD_W), lambda h: (0, h)),
            pl.BlockSpec((VT_ROWS, s), lambda h: (h, 0), pipeline_mode=pl.Buffered(1)),
        ],
        out_specs=pl.BlockSpec((HEAD_W, s), lambda h: (h, 0)),
        out_shape=jax.ShapeDtypeStruct((width, s), BF16),
        scratch_shapes=[
            pltpu.VMEM((2, 1, tile), F32),
            pltpu.VMEM((2, 2, VT_ROWS, tile), F32),
            pltpu.VMEM((2, tile, tile), F32),
            pltpu.VMEM((2, 2, tile, tile), BF16),
            pltpu.VMEM((2, 2, 1, tile), F32),
            pltpu.VMEM((2, 1, tile), F32),
        ],
        compiler_params=pltpu.CompilerParams(
            dimension_semantics=("arbitrary",), vmem_limit_bytes=V7X_VMEM_LIMIT_BYTES),
        name="diff_attn",
    )(lam_p, g_sub, q, k, vt)


def _swiglu(h, wg_ref, wu_ref, wd_ref, act_sc, rows):
    hb = h.astype(BF16)
    d_ff = wg_ref.shape[1]
    for c in range(d_ff // MXU_WIDTH):
        cols = slice(c * MXU_WIDTH, (c + 1) * MXU_WIDTH)
        gate = jnp.dot(hb, wg_ref[:, cols], preferred_element_type=F32)
        up = jnp.dot(hb, wu_ref[:, cols], preferred_element_type=F32)
        act_sc[rows, cols] = (gate / (1.0 + jnp.exp(-gate)) * up).astype(BF16)
    return jnp.dot(act_sc[rows, :], wd_ref[...], preferred_element_type=F32)


def _row_groups(tm):
    rows = tm // ROW_GROUPS
    return [slice(g * rows, (g + 1) * rows) for g in range(ROW_GROUPS)]


def _mixer_out_ffn(resid, project, wg_ref, wu_ref, wd_ref, ln, out_ref, act_sc, groups):
    m = [project(r) for r in groups]
    h = [_layer_norm(ALPHA * resid(r) + m_g, ln[0:1], ln[1:2]) for r, m_g in zip(groups, m)]
    y = [_swiglu(h_g, wg_ref, wu_ref, wd_ref, act_sc, r) for r, h_g in zip(groups, h)]
    for r, h_g, y_g in zip(groups, h, y):
        out_ref[r, :] = _layer_norm(ALPHA * h_g + y_g, ln[2:3], ln[3:4])


def _post_attn_kernel(x_ref, o_ref, wo_ref, wg_ref, wu_ref, wd_ref, ln_ref, out_ref, act_sc):
    ln = ln_ref[...]
    def project(r):
        return lax.dot_general(o_ref[:, r], wo_ref[...], (((0,), (0,)), ((), ())), preferred_element_type=F32)

    _mixer_out_ffn(lambda r: x_ref[r, :], project, wg_ref, wu_ref, wd_ref, ln,
                   out_ref, act_sc, _row_groups(x_ref.shape[0]))


def _post_attn_call(x2, o, w_o, w_gate, w_up, w_down, ln, *, tm, layer):
    s, d = x2.shape
    d_ff = w_gate.shape[2]
    return pl.pallas_call(
        _post_attn_kernel,
        grid=(s // tm,),
        in_specs=[
            pl.BlockSpec((tm, d), lambda i: (i, 0)),
            pl.BlockSpec((o.shape[0], tm), lambda i: (0, i)),
            _resident_layer(w_o, 0),
            _resident_layer(w_gate, layer),
            _resident_layer(w_up, layer),
            _resident_layer(w_down, layer),
            _resident(ln.shape),
        ],
        out_specs=pl.BlockSpec((tm, d), lambda i: (i, 0)),
        out_shape=jax.ShapeDtypeStruct((s, d), F32),
        scratch_shapes=[pltpu.VMEM((tm, d_ff), BF16)],
        compiler_params=pltpu.CompilerParams(
            dimension_semantics=("arbitrary",), vmem_limit_bytes=V7X_VMEM_LIMIT_BYTES),
        name="attn_out_ffn",
    )(x2, o, w_o, w_gate, w_up, w_down, ln)


def _conv_layer_kernel(h_ref, win_ref, wc_ref, wout_ref, wg_ref, wu_ref, wd_ref, ln_ref, out_ref,
                       act_sc, tail_sc, *, tm, ch):
    i = pl.program_id(0)
    ln = ln_ref[...]
    h = h_ref[...]
    hb = h.astype(BF16)

    @pl.when(i == 0)
    def _():
        tail_sc[...] = jnp.zeros(tail_sc.shape, F32)

    b_gate = jnp.dot(hb, win_ref[:, 0:ch], preferred_element_type=F32)
    c_gate = jnp.dot(hb, win_ref[:, ch:2 * ch], preferred_element_type=F32)
    xt = jnp.dot(hb, win_ref[:, 2 * ch:3 * ch], preferred_element_type=F32)
    u = c_gate * xt

    row = lax.broadcasted_iota(jnp.int32, (tm, ch), 0)
    tail = tail_sc[...]
    wc = wc_ref[...]
    y = wc[CONV_WIDTH - 1:CONV_WIDTH] * u
    for d in range(1, CONV_WIDTH):
        shifted = pltpu.roll(u, d, 0)
        for t in range(d):
            src = CONV_WIDTH - 1 - d + t
            shifted = jnp.where(row == t, tail[src:src + 1], shifted)
        y = y + wc[CONV_WIDTH - 1 - d:CONV_WIDTH - d] * shifted
    tail_sc[...] = u[tm - (CONV_WIDTH - 1):]

    mixed = (b_gate * y).astype(BF16)
    def project(r):
        return jnp.dot(mixed[r, :], wout_ref[...], preferred_element_type=F32)

    _mixer_out_ffn(lambda r: h[r, :], project, wg_ref, wu_ref, wd_ref, ln,
                   out_ref, act_sc, _row_groups(tm))


def _conv_layer_call(h, w_in, w_conv, w_out, w_gate, w_up, w_down, ln, *, tm, layer):
    s, d = h.shape
    ch = w_out.shape[1]
    d_ff = w_gate.shape[2]
    return pl.pallas_call(
        functools.partial(_conv_layer_kernel, tm=tm, ch=ch),
        grid=(s // tm,),
        in_specs=[
            pl.BlockSpec((tm, d), lambda i: (i, 0)),
            _resident_layer(w_in, 0),
            _resident(w_conv.shape),
            _resident_layer(w_out, 0),
            _resident_layer(w_gate, layer),
            _resident_layer(w_up, layer),
            _resident_layer(w_down, layer),
            _resident(ln.shape),
        ],
        out_specs=pl.BlockSpec((tm, d), lambda i: (i, 0)),
        out_shape=jax.ShapeDtypeStruct((s, d), F32),
        scratch_shapes=[pltpu.VMEM((tm, d_ff), BF16), pltpu.VMEM((CONV_WIDTH - 1, ch), F32)],
        compiler_params=pltpu.CompilerParams(
            dimension_semantics=("arbitrary",), vmem_limit_bytes=V7X_VMEM_LIMIT_BYTES),
        name="conv_ffn",
    )(h, w_in, w_conv, w_out, w_gate, w_up, w_down, ln)


def _rope_inv_lanes():
    inv = ROPE_THETA ** (-jnp.arange(0, ROPE_DIM, 2, dtype=F32) / ROPE_DIM)
    per_comp = jnp.concatenate([inv, inv, jnp.zeros((HEAD_DIM - ROPE_DIM,), F32)])
    return jnp.tile(per_comp, HEAD_W // HEAD_DIM)[None, :]


def kernel(x, attn_w_qkv, attn_w_o, attn_lambda_q1, attn_lambda_k1, attn_lambda_q2, attn_lambda_k2,
           attn_subln_g, conv_w_in, conv_w, conv_w_out, ffn_w_gate, ffn_w_up, ffn_w_down,
           ln1_g, ln1_b, ln2_g, ln2_b):
    b, s, d = x.shape
    assert b == 1 and attn_w_qkv.shape[0] == 1 and conv_w_in.shape[0] == 1
    tm = min(512, s)
    tile = min(1024, s)
    x2 = x.reshape(s, d)

    later = (attn_w_o, conv_w_in, conv_w_out, ffn_w_gate, ffn_w_up, ffn_w_down)
    (qt, k, vt), (w_o, w_in, w_out, w_gate, w_up, w_down) = _qkv_call(
        x2, attn_w_qkv, _rope_inv_lanes(), later, tm=tm)
    lam_p = jnp.concatenate([attn_lambda_q1, attn_lambda_k1, attn_lambda_q2, attn_lambda_k2], axis=0)
    o = _attn_call(lam_p, attn_subln_g.reshape(HEAD_W, 1), qt, k, vt, tile=tile, lam_init=LAMBDA_INIT_0)

    ln0 = jnp.stack([ln1_g[0], ln1_b[0], ln2_g[0], ln2_b[0]])
    h = _post_attn_call(x2, o, w_o, w_gate, w_up, w_down, ln0, tm=tm, layer=0)

    ln1 = jnp.stack([ln1_g[1], ln1_b[1], ln2_g[1], ln2_b[1]])
    out = _conv_layer_call(h, w_in, conv_w[0], w_out, w_gate, w_up, w_down, ln1, tm=tm, layer=1)
    return out.reshape(b, s, d)
```

```python
import functools
import math

import jax
import jax.numpy as jnp
from jax import lax
from jax.experimental import pallas as pl
from jax.experimental.pallas import tpu as pltpu

F32 = jnp.float32
BF16 = jnp.bfloat16

CHUNK = 64
N_HEADS = 8
HEAD_DIM = 64
HEAD_W = 2 * HEAD_DIM
ROPE_DIM = HEAD_DIM // 4
ROPE_HALF = ROPE_DIM // 2
ROPE_THETA = 500000.0
CONV_WIDTH = 3
DEPTH = 2
ALPHA = (2 * DEPTH) ** 0.25
LN_EPS = 1e-5
NEG_INF = -1e30
LAMBDA_INIT_0 = 0.8 - 0.6 * math.exp(-0.3 * 0)

V7X_VMEM_LIMIT_BYTES = 56 * 1024 * 1024
MXU_WIDTH = 256
LANES = 128
ROW_GROUPS = 2
BF16_SUBLANES = 16
VT_ROWS = HEAD_W + BF16_SUBLANES
Q_SCALE = HEAD_DIM ** -0.5 * math.log2(math.e)


def _resident(shape):
    return pl.BlockSpec(shape, lambda *_: (0,) * len(shape), pipeline_mode=pl.Buffered(1))


def _resident_layer(stacked, layer):
    return pl.BlockSpec((None,) + stacked.shape[1:], lambda *_: (layer, 0, 0), pipeline_mode=pl.Buffered(1))


def _cast_spec(stacked, n_steps):
    rows = stacked.shape[1]
    n_blk = max(d for d in range(1, n_steps + 1) if rows % d == 0 and (rows // d) % BF16_SUBLANES == 0)
    block = (stacked.shape[0], rows // n_blk, stacked.shape[2])
    return pl.BlockSpec(block, lambda i: (0, i * n_blk // n_steps, 0))


def _layer_norm(x, g, b):
    mu = jnp.mean(x, axis=-1, keepdims=True)
    xc = x - mu
    var = jnp.mean(xc * xc, axis=-1, keepdims=True)
    return xc * lax.rsqrt(var + LN_EPS) * g + b


def _qkv_kernel(x_ref, w_ref, inv_ref, *rest, tm, width, n_cast):
    cast_src, (qt_ref, k_ref, vt_ref), cast_dst, (cos_sc, sin_sc, wq_sc) = (
        rest[:n_cast], rest[n_cast:n_cast + 3], rest[n_cast + 3:2 * n_cast + 3], rest[2 * n_cast + 3:])
    i = pl.program_id(0)
    xb = x_ref[...].astype(BF16)
    inv = inv_ref[...]

    for src, dst in zip(cast_src, cast_dst):
        dst[...] = src[...].astype(BF16)

    @pl.when(i == 0)
    def _():
        wq_sc[...] = w_ref[...].astype(BF16)
        off = lax.broadcasted_iota(jnp.int32, (tm, HEAD_W), 0).astype(F32) * inv
        cos_sc[...] = jnp.cos(off)
        sin_sc[...] = jnp.sin(off)

    base = (i * tm).astype(F32) * inv
    cos_b, sin_b = jnp.cos(base), jnp.sin(base)
    cos_r, sin_r = cos_sc[...], sin_sc[...]
    cos = cos_b * cos_r - sin_b * sin_r
    sin = sin_b * cos_r + cos_b * sin_r
    lane = lax.broadcasted_iota(jnp.int32, (tm, HEAD_W), 1)
    w = lane % HEAD_DIM
    c_tab = jnp.where(w < ROPE_DIM, cos, 1.0)
    s_up = jnp.where(w < ROPE_HALF, -sin, 0.0)
    s_dn = jnp.where((w >= ROPE_HALF) & (w < ROPE_DIM), sin, 0.0)
    first_t = lax.broadcasted_iota(jnp.int32, (HEAD_W, tm), 0) < HEAD_DIM

    def rope(t):
        up = pltpu.roll(t, HEAD_W - ROPE_HALF, 1)
        dn = pltpu.roll(t, ROPE_HALF, 1)
        return t * c_tab + up * s_up + dn * s_dn

    def proj(col):
        return jnp.dot(xb, wq_sc[:, col:col + MXU_WIDTH], preferred_element_type=F32)

    ones_row = (lax.broadcasted_iota(jnp.int32, (BF16_SUBLANES, tm), 0) == 0).astype(BF16)

    for c in range(width // MXU_WIDTH):
        q2 = proj(c * MXU_WIDTH)
        k2 = proj(width + c * MXU_WIDTH)
        v2 = proj(2 * width + c * MXU_WIDTH)
        for half in range(MXU_WIDTH // HEAD_W):
            head = c * (MXU_WIDTH // HEAD_W) + half
            lo = head * HEAD_W
            cols = slice(half * HEAD_W, (half + 1) * HEAD_W)
            qh_t = (rope(q2[:, cols]) * Q_SCALE).T
            qt_ref[0, lo:lo + HEAD_W, :] = jnp.where(first_t, qh_t, 0.0).astype(BF16)
            qt_ref[1, lo:lo + HEAD_W, :] = jnp.where(first_t, 0.0, qh_t).astype(BF16)
            k_ref[:, lo:lo + HEAD_W] = rope(k2[:, cols]).astype(BF16)
            vlo = head * VT_ROWS
            vt_ref[vlo:vlo + HEAD_W, :] = v2[:, cols].T.astype(BF16)
            vt_ref[vlo + HEAD_W:vlo + VT_ROWS, :] = ones_row


def _qkv_call(x2, w_qkv, inv_lane, later_weights, *, tm):
    s, d = x2.shape
    width = w_qkv.shape[2] // 3
    n_steps = s // tm
    cast_specs = [_cast_spec(w, n_steps) for w in later_weights]
    outs = pl.pallas_call(
        functools.partial(_qkv_kernel, tm=tm, width=width, n_cast=len(later_weights)),
        grid=(n_steps,),
        in_specs=[
            pl.BlockSpec((tm, d), lambda i: (i, 0)),
            _resident_layer(w_qkv, 0),
            _resident(inv_lane.shape),
            *cast_specs,
        ],
        out_specs=[
            pl.BlockSpec((2, width, tm), lambda i: (0, 0, i)),
            pl.BlockSpec((tm, width), lambda i: (i, 0)),
            pl.BlockSpec((N_HEADS * VT_ROWS, tm), lambda i: (0, i)),
            *cast_specs,
        ],
        out_shape=[
            jax.ShapeDtypeStruct((2, width, s), BF16),
            jax.ShapeDtypeStruct((s, width), BF16),
            jax.ShapeDtypeStruct((N_HEADS * VT_ROWS, s), BF16),
            *[jax.ShapeDtypeStruct(w.shape, BF16) for w in later_weights],
        ],
        scratch_shapes=[pltpu.VMEM((tm, HEAD_W), F32), pltpu.VMEM((tm, HEAD_W), F32),
                        pltpu.VMEM(w_qkv.shape[1:], BF16)],
        compiler_params=pltpu.CompilerParams(
            dimension_semantics=("arbitrary",), vmem_limit_bytes=V7X_VMEM_LIMIT_BYTES),
        name="qkv_rope",
    )(x2, w_qkv, inv_lane, *later_weights)
    return outs[:3], outs[3:]


def _attn_kernel(lam_ref, g_ref, qt_ref, k_ref, vt_ref, o_ref, m_sc, acc_sc, s_sc, p_sc, a_sc, smax_sc,
                 *, tile, n_blocks, lam_init):
    n_pairs = n_blocks * (n_blocks + 1) // 2
    n_steps = n_pairs // 2 + 1

    m_sc[...] = jnp.full(m_sc.shape, NEG_INF, F32)
    acc_sc[...] = jnp.zeros(acc_sc.shape, F32)
    p_sc[...] = jnp.zeros(p_sc.shape, BF16)
    a_sc[...] = jnp.ones(a_sc.shape, F32)

    key = lax.broadcasted_iota(jnp.int32, (tile, tile), 0)
    qry = lax.broadcasted_iota(jnp.int32, (tile, tile), 1)
    diag_mask = key // CHUNK <= qry // CHUNK

    def clamped(x):
        return jnp.minimum(x, n_blocks - 1)

    def scores(i, t):
        k_t = k_ref[pl.ds(pl.multiple_of(t * tile, tile), tile), :]
        q_lo = pl.multiple_of(i * tile, tile)
        for c in range(2):
            s = jnp.dot(k_t, qt_ref[c, :, pl.ds(q_lo, tile)], preferred_element_type=F32)
            s_sc[c] = s
            smax_sc[c] = jnp.max(s, axis=0, keepdims=True)

    def softmax(slot, t, masked):
        for c in range(2):
            s = s_sc[c]
            if masked:
                s = jnp.where(diag_mask, s, NEG_INF)
                s_max = jnp.max(s, axis=0, keepdims=True)
            else:
                s_max = smax_sc[c]
            m_prev = jnp.where(t == 0, NEG_INF, m_sc[c])
            m_new = jnp.maximum(m_prev, s_max)
            p_sc[slot, c] = jnp.exp2(s - m_new).astype(BF16)
            a_sc[slot, c] = jnp.exp2(m_prev - m_new)
            m_sc[c] = m_new

    def accumulate(slot, i, t):
        v_t = vt_ref[:, pl.ds(pl.multiple_of(clamped(t) * tile, tile), tile)]
        par = i % 2
        for c in range(2):
            acc_sc[par, c] = a_sc[slot, c] * acc_sc[par, c] + jnp.dot(
                v_t, p_sc[slot, c], preferred_element_type=F32)

    def finalize(i):
        lam_p = lam_ref[...]
        lam = (jnp.exp(jnp.sum(lam_p[0:1] * lam_p[1:2], axis=1, keepdims=True))
               - jnp.exp(jnp.sum(lam_p[2:3] * lam_p[3:4], axis=1, keepdims=True)) + lam_init)
        par = i % 2

        g_b = jnp.broadcast_to(g_ref[...] * (1.0 - lam_init), (HEAD_W, LANES))
        for j in range(tile // LANES):
            cols = slice(j * LANES, (j + 1) * LANES)

            def normalised(c):
                return acc_sc[par, c, :HEAD_W, cols] / acc_sc[par, c, HEAD_W:HEAD_W + 1, cols]

            o_t = normalised(0) - lam * normalised(1)
            o_t = o_t * lax.rsqrt(jnp.mean(o_t * o_t, axis=0, keepdims=True) + LN_EPS) * g_b
            o_ref[:, pl.ds(pl.multiple_of(i * tile + j * LANES, LANES), LANES)] = o_t.astype(o_ref.dtype)

    def following(i, t):
        last = t == i
        return jnp.where(last, i + 1, i), jnp.where(last, 0, t + 1)

    scores(0, 0)

    def step(_, carry):
        i3, t3, i2, t2, i1, t1, i0, t0, done = carry

        @pl.when(done >= 0)
        def _():
            finalize(done)

        def stages(diag_a, diag_b):
            accumulate(1, i3, t3)
            softmax(0, t2, diag_a)
            scores(clamped(i1), clamped(t1))
            accumulate(0, i2, t2)
            softmax(1, t1, diag_b)
            scores(clamped(i0), clamped(t0))

        variant = jnp.where(t2 == i2, 1, jnp.where(t1 == i1, 2, 0))
        lax.switch(variant, [lambda: stages(False, False), lambda: stages(True, False),
                             lambda: stages(False, True)])
        done = jnp.where(t3 == i3, i3, jnp.where(t2 == i2, i2, -1))
        done = jnp.where(done < n_blocks, done, -1)
        nxt = following(i0, t0)
        return (i1, t1, i0, t0) + nxt + following(*nxt) + (done,)

    zero = jnp.int32(0)
    first = (zero, zero)
    second = following(*first)
    carry = lax.fori_loop(0, n_steps, step, first + first + second + following(*second) + (jnp.int32(-1),))

    @pl.when(carry[-1] >= 0)
    def _():
        finalize(carry[-1])


def _attn_call(lam_p, g_sub, qt, k, vt, *, tile, lam_init):
    s, width = k.shape
    return pl.pallas_call(
        functools.partial(_attn_kernel, tile=tile, n_blocks=s // tile, lam_init=lam_init),
        grid=(N_HEADS,),
        in_specs=[
            pl.BlockSpec(lam_p.shape, lambda h: (0, 0)),
            pl.BlockSpec(g_sub.shape, lambda h: (0, 0)),
            pl.BlockSpec((2, HEAD_W, s), lambda h: (0, h, 0), pipeline_mode=pl.Buffered(1)),
            pl.BlockSpec((s, HEAD_W), lambda h: (0, h)),
            pl.BlockSpec((VT_ROWS, s), lambda h: (h, 0), pipeline_mode=pl.Buffered(1)),
        ],
        out_specs=pl.BlockSpec((HEAD_W, s), lambda h: (h, 0)),
        out_shape=jax.ShapeDtypeStruct((width, s), BF16),
        scratch_shapes=[
            pltpu.VMEM((2, 1, tile), F32),
            pltpu.VMEM((2, 2, VT_ROWS, tile), F32),
            pltpu.VMEM((2, tile, tile), F32),
            pltpu.VMEM((2, 2, tile, tile), BF16),
            pltpu.VMEM((2, 2, 1, tile), F32),
            pltpu.VMEM((2, 1, tile), F32),
        ],
        compiler_params=pltpu.CompilerParams(
            dimension_semantics=("arbitrary",), vmem_limit_bytes=V7X_VMEM_LIMIT_BYTES),
        name="diff_attn",
    )(lam_p, g_sub, qt, k, vt)


def _swiglu(h, wg_ref, wu_ref, wd_ref, act_sc, rows):
    hb = h.astype(BF16)
    d_ff = wg_ref.shape[1]
    for c in range(d_ff // MXU_WIDTH):
        cols = slice(c * MXU_WIDTH, (c + 1) * MXU_WIDTH)
        gate = jnp.dot(hb, wg_ref[:, cols], preferred_element_type=F32)
        up = jnp.dot(hb, wu_ref[:, cols], preferred_element_type=F32)
        act_sc[rows, cols] = (gate / (1.0 + jnp.exp(-gate)) * up).astype(BF16)
    return jnp.dot(act_sc[rows, :], wd_ref[...], preferred_element_type=F32)


def _row_groups(tm):
    rows = tm // ROW_GROUPS
    return [slice(g * rows, (g + 1) * rows) for g in range(ROW_GROUPS)]


def _mixer_out_ffn(resid, project, wg_ref, wu_ref, wd_ref, ln, out_ref, act_sc, groups):
    m = [project(r) for r in groups]
    h = [_layer_norm(ALPHA * resid(r) + m_g, ln[0:1], ln[1:2]) for r, m_g in zip(groups, m)]
    y = [_swiglu(h_g, wg_ref, wu_ref, wd_ref, act_sc, r) for r, h_g in zip(groups, h)]
    for r, h_g, y_g in zip(groups, h, y):
        out_ref[r, :] = _layer_norm(ALPHA * h_g + y_g, ln[2:3], ln[3:4])


def _post_attn_kernel(x_ref, o_ref, wo_ref, wg_ref, wu_ref, wd_ref, ln_ref, out_ref, act_sc):
    ln = ln_ref[...]
    def project(r):
        return lax.dot_general(o_ref[:, r], wo_ref[...], (((0,), (0,)), ((), ())), preferred_element_type=F32)

    _mixer_out_ffn(lambda r: x_ref[r, :], project, wg_ref, wu_ref, wd_ref, ln,
                   out_ref, act_sc, _row_groups(x_ref.shape[0]))


def _post_attn_call(x2, o, w_o, w_gate, w_up, w_down, ln, *, tm, layer):
    s, d = x2.shape
    d_ff = w_gate.shape[2]
    return pl.pallas_call(
        _post_attn_kernel,
        grid=(s // tm,),
        in_specs=[
            pl.BlockSpec((tm, d), lambda i: (i, 0)),
            pl.BlockSpec((o.shape[0], tm), lambda i: (0, i)),
            _resident_layer(w_o, 0),
            _resident_layer(w_gate, layer),
            _resident_layer(w_up, layer),
            _resident_layer(w_down, layer),
            _resident(ln.shape),
        ],
        out_specs=pl.BlockSpec((tm, d), lambda i: (i, 0)),
        out_shape=jax.ShapeDtypeStruct((s, d), F32),
        scratch_shapes=[pltpu.VMEM((tm, d_ff), BF16)],
        compiler_params=pltpu.CompilerParams(
            dimension_semantics=("arbitrary",), vmem_limit_bytes=V7X_VMEM_LIMIT_BYTES),
        name="attn_out_ffn",
    )(x2, o, w_o, w_gate, w_up, w_down, ln)


def _conv_layer_kernel(h_ref, win_ref, wc_ref, wout_ref, wg_ref, wu_ref, wd_ref, ln_ref, out_ref,
                       act_sc, tail_sc, *, tm, ch):
    i = pl.program_id(0)
    ln = ln_ref[...]
    h = h_ref[...]
    hb = h.astype(BF16)

    @pl.when(i == 0)
    def _():
        tail_sc[...] = jnp.zeros(tail_sc.shape, F32)

    b_gate = jnp.dot(hb, win_ref[:, 0:ch], preferred_element_type=F32)
    c_gate = jnp.dot(hb, win_ref[:, ch:2 * ch], preferred_element_type=F32)
    xt = jnp.dot(hb, win_ref[:, 2 * ch:3 * ch], preferred_element_type=F32)
    u = c_gate * xt

    row = lax.broadcasted_iota(jnp.int32, (tm, ch), 0)
    tail = tail_sc[...]
    wc = wc_ref[...]
    y = wc[CONV_WIDTH - 1:CONV_WIDTH] * u
    for d in range(1, CONV_WIDTH):
        shifted = pltpu.roll(u, d, 0)
        for t in range(d):
            src = CONV_WIDTH - 1 - d + t
            shifted = jnp.where(row == t, tail[src:src + 1], shifted)
        y = y + wc[CONV_WIDTH - 1 - d:CONV_WIDTH - d] * shifted
    tail_sc[...] = u[tm - (CONV_WIDTH - 1):]

    mixed = (b_gate * y).astype(BF16)
    def project(r):
        return jnp.dot(mixed[r, :], wout_ref[...], preferred_element_type=F32)

    _mixer_out_ffn(lambda r: h[r, :], project, wg_ref, wu_ref, wd_ref, ln,
                   out_ref, act_sc, _row_groups(tm))


def _conv_layer_call(h, w_in, w_conv, w_out, w_gate, w_up, w_down, ln, *, tm, layer):
    s, d = h.shape
    ch = w_out.shape[1]
    d_ff = w_gate.shape[2]
    return pl.pallas_call(
        functools.partial(_conv_layer_kernel, tm=tm, ch=ch),
        grid=(s // tm,),
        in_specs=[
            pl.BlockSpec((tm, d), lambda i: (i, 0)),
            _resident_layer(w_in, 0),
            _resident(w_conv.shape),
            _resident_layer(w_out, 0),
            _resident_layer(w_gate, layer),
            _resident_layer(w_up, layer),
            _resident_layer(w_down, layer),
            _resident(ln.shape),
        ],
        out_specs=pl.BlockSpec((tm, d), lambda i: (i, 0)),
        out_shape=jax.ShapeDtypeStruct((s, d), F32),
        scratch_shapes=[pltpu.VMEM((tm, d_ff), BF16), pltpu.VMEM((CONV_WIDTH - 1, ch), F32)],
        compiler_params=pltpu.CompilerParams(
            dimension_semantics=("arbitrary",), vmem_limit_bytes=V7X_VMEM_LIMIT_BYTES),
        name="conv_ffn",
    )(h, w_in, w_conv, w_out, w_gate, w_up, w_down, ln)


def _rope_inv_lanes():
    inv = ROPE_THETA ** (-jnp.arange(0, ROPE_DIM, 2, dtype=F32) / ROPE_DIM)
    per_comp = jnp.concatenate([inv, inv, jnp.zeros((HEAD_DIM - ROPE_DIM,), F32)])
    return jnp.tile(per_comp, HEAD_W // HEAD_DIM)[None, :]


def kernel(x, attn_w_qkv, attn_w_o, attn_lambda_q1, attn_lambda_k1, attn_lambda_q2, attn_lambda_k2,
           attn_subln_g, conv_w_in, conv_w, conv_w_out, ffn_w_gate, ffn_w_up, ffn_w_down,
           ln1_g, ln1_b, ln2_g, ln2_b):
    b, s, d = x.shape
    assert b == 1 and attn_w_qkv.shape[0] == 1 and conv_w_in.shape[0] == 1
    tm = min(512, s)
    tile = min(1024, s)
    x2 = x.reshape(s, d)

    later = (attn_w_o, conv_w_in, conv_w_out, ffn_w_gate, ffn_w_up, ffn_w_down)
    (qt, k, vt), (w_o, w_in, w_out, w_gate, w_up, w_down) = _qkv_call(
        x2, attn_w_qkv, _rope_inv_lanes(), later, tm=tm)
    lam_p = jnp.concatenate([attn_lambda_q1, attn_lambda_k1, attn_lambda_q2, attn_lambda_k2], axis=0)
    o = _attn_call(lam_p, attn_subln_g.reshape(HEAD_W, 1), qt, k, vt, tile=tile, lam_init=LAMBDA_INIT_0)

    ln0 = jnp.stack([ln1_g[0], ln1_b[0], ln2_g[0], ln2_b[0]])
    h = _post_attn_call(x2, o, w_o, w_gate, w_up, w_down, ln0, tm=tm, layer=0)

    ln1 = jnp.stack([ln1_g[1], ln1_b[1], ln2_g[1], ln2_b[1]])
    out = _conv_layer_call(h, w_in, conv_w[0], w_out, w_gate, w_up, w_down, ln1, tm=tm, layer=1)
    return out.reshape(b, s, d)
```
